```python
import math
import jax
import jax.numpy as jnp
from jax import lax
import numpy as np

D_MODEL = 1024
BATCH = 4
SEQ = 4096
DEPTH = 4

N_META = 16
CHUNK = 64
META_PAD = CHUNK - N_META
CONV_K = 4
N_BRANCH = 3
BRANCH_WIDTH = 768

S5_WIDTH = BRANCH_WIDTH
S5_GROUP = 16
S5_GROUPS = S5_WIDTH // S5_GROUP
S5_STATE = 64
S5_STEP_MIN = 1e-3
S5_STEP_MAX = 1e-1

SSD_HEAD_DIM = 64
SSD_HEADS = BRANCH_WIDTH // SSD_HEAD_DIM
SSD_WIDTH = SSD_HEADS * SSD_HEAD_DIM
SSD_GROUPS = 2
SSD_STATE = 128
SSD_CONV_WIDTH = SSD_WIDTH + 2 * SSD_GROUPS * SSD_STATE

GDN_HEAD_DIM = 128
GDN_HEADS = BRANCH_WIDTH // GDN_HEAD_DIM
GDN_WIDTH = GDN_HEADS * GDN_HEAD_DIM

IN_SPLITS = (S5_WIDTH, S5_WIDTH,
             SSD_CONV_WIDTH, SSD_HEADS, SSD_WIDTH,
             3 * GDN_WIDTH, GDN_HEADS, GDN_HEADS, GDN_WIDTH,
             N_BRANCH * D_MODEL)
IN_WIDTH = sum(IN_SPLITS)

ALPHA = (2 * DEPTH) ** 0.25
BETA = (8 * DEPTH) ** -0.25
LN_EPS = 1e-5

kernel_name = 'hybrid_s5_ssd_gdn_gated_merge'


def _split_points():
    pts, acc = [], 0
    for w in IN_SPLITS[:-1]:
        acc += w
        pts.append(acc)
    return pts


def _layer_norm(z, g, b):
    zf = z.astype(jnp.float32)
    mu = jnp.mean(zf, axis=-1, keepdims=True)
    var = jnp.mean(jnp.square(zf - mu), axis=-1, keepdims=True)
    return ((zf - mu) * lax.rsqrt(var + LN_EPS) * g + b).astype(z.dtype)


def _rms_norm(z, g):
    zf = z.astype(jnp.float32)
    return zf * lax.rsqrt(jnp.mean(zf * zf, axis=-1, keepdims=True) + LN_EPS) * g.astype(jnp.float32)


def _l2norm(z):
    return z * lax.rsqrt(jnp.sum(z * z, axis=-1, keepdims=True) + 1e-6)


def _front_pad(z):
    return jnp.pad(z, [(0, 0), (META_PAD, 0)] + [(0, 0)] * (z.ndim - 2))


def _causal_dwconv(u, w):
    k, ch = w.shape
    return lax.conv_general_dilated(u, w.astype(u.dtype)[:, None, :], window_strides=(1,),
                                    padding=[(k - 1, 0)], dimension_numbers=('NWC', 'WIO', 'NWC'),
                                    feature_group_count=ch)


def _complex_affine_combine(earlier, later):
    a1r, a1i, b1r, b1i = earlier
    a2r, a2i, b2r, b2i = later
    return (a2r * a1r - a2i * a1i, a2r * a1i + a2i * a1r,
            a2r * b1r - a2i * b1i + b2r, a2r * b1i + a2i * b1r + b2i)


def _s5_branch(u, z, a_re, a_im, log_step, b_re, b_im, c_re, c_im, d, w_glu, b_glu):
    bsz, t, _ = u.shape
    uf = u.astype(jnp.float32)
    ug = uf.reshape(bsz, t, S5_GROUPS, S5_GROUP)
    lam_re = jnp.minimum(a_re.astype(jnp.float32), -1e-4)
    lam_im = a_im.astype(jnp.float32)
    step = jnp.exp(log_step.astype(jnp.float32))[:, None]
    mag = jnp.exp(lam_re * step)
    abar_re, abar_im = mag * jnp.cos(lam_im * step), mag * jnp.sin(lam_im * step)
    den = lam_re * lam_re + lam_im * lam_im
    nr, ni = abar_re - 1.0, abar_im
    coef_re = (nr * lam_re + ni * lam_im) / den
    coef_im = (ni * lam_re - nr * lam_im) / den
    bbar_re = coef_re[..., None] * b_re - coef_im[..., None] * b_im
    bbar_im = coef_re[..., None] * b_im + coef_im[..., None] * b_re
    bu_re = jnp.einsum('btgc,gpc->btgp', ug, bbar_re)
    bu_im = jnp.einsum('btgc,gpc->btgp', ug, bbar_im)
    ae_re = jnp.broadcast_to(abar_re, (1, t, S5_GROUPS, S5_STATE))
    ae_im = jnp.broadcast_to(abar_im, (1, t, S5_GROUPS, S5_STATE))
    _, _, s_re, s_im = lax.associative_scan(_complex_affine_combine, (ae_re, ae_im, bu_re, bu_im), axis=1)
    y = jnp.einsum('btgp,gcp->btgc', s_re, c_re) - jnp.einsum('btgp,gcp->btgc', s_im, c_im)
    y = y.reshape(bsz, t, S5_WIDTH) + d * uf
    v = jax.nn.gelu(y)
    v = v * jax.nn.sigmoid(v @ w_glu + b_glu)
    return (v * jax.nn.silu(z.astype(jnp.float32))).astype(u.dtype)


def _ssd_chunked(x, dt, a, b, c):
    bsz, t, h, p = x.shape
    g, n = b.shape[2], b.shape[3]
    r = h // g
    nc = t // CHUNK
    xd = (x * dt[..., None]).reshape(bsz, nc, CHUNK, g, r, p)
    acum = jnp.cumsum((dt * a).reshape(bsz, nc, CHUNK, g, r), axis=2)
    bc = b.reshape(bsz, nc, CHUNK, g, n)
    cc = c.reshape(bsz, nc, CHUNK, g, n)
    causal = jnp.tril(jnp.ones((CHUNK, CHUNK), dtype=bool))
    seg = acum[:, :, :, None] - acum[:, :, None]
    decay = jnp.exp(jnp.where(causal[:, :, None, None], seg, -jnp.inf))
    scores = jnp.einsum('bclgn,bcsgn->bclsg', cc, bc)[..., None] * decay
    y_diag = jnp.einsum('bclsgr,bcsgrp->bclgrp', scores, xd)
    to_end = jnp.exp(acum[:, :, -1:] - acum)
    states = jnp.einsum('bcsgn,bcsgr,bcsgrp->bcgrpn', bc, to_end, xd)
    chunk_decay = jnp.exp(acum[:, :, -1])

    def step(state, inp):
        st, dec = inp
        return state * dec[..., None, None] + st, state

    init = jnp.zeros((bsz, g, r, p, n), x.dtype)
    _, prev = lax.scan(step, init, (jnp.moveaxis(states, 1, 0), jnp.moveaxis(chunk_decay, 1, 0)))
    y_off = jnp.einsum('bclgn,cbgrpn,bclgr->bclgrp', cc, prev, jnp.exp(acum))
    return (y_diag + y_off).reshape(bsz, t, h, p)


def _ssd_branch(xbc, dt_raw, z, conv_w, conv_b, dt_bias, a_log, d, norm_g):
    bsz, t, _ = xbc.shape
    xbc = jax.nn.silu(_causal_dwconv(xbc, conv_w) + conv_b).astype(jnp.float32)
    xs, bs, cs = jnp.split(xbc, [SSD_WIDTH, SSD_WIDTH + SSD_GROUPS * SSD_STATE], axis=-1)
    xs = xs.reshape(bsz, t, SSD_HEADS, SSD_HEAD_DIM)
    bs = bs.reshape(bsz, t, SSD_GROUPS, SSD_STATE)
    cs = cs.reshape(bsz, t, SSD_GROUPS, SSD_STATE)
    dt = jax.nn.softplus(dt_raw.astype(jnp.float32) + dt_bias.astype(jnp.float32))
    a = -jnp.exp(a_log.astype(jnp.float32))
    y = _ssd_chunked(_front_pad(xs), _front_pad(dt), a, _front_pad(bs), _front_pad(cs))[:, META_PAD:]
    y = y + xs * d.astype(jnp.float32)[:, None]
    y = y.reshape(bsz, t, SSD_WIDTH) * jax.nn.silu(z.astype(jnp.float32))
    return _rms_norm(y, norm_g).astype(z.dtype)


def _gated_delta_chunked(q, k, v, beta, g):
    bsz, t, h, dk = k.shape
    dv = v.shape[-1]
    nc = t // CHUNK
    q, k, v, beta, g = (u.reshape(bsz, nc, CHUNK, *u.shape[2:]) for u in (q, k, v, beta, g))
    gcum = jnp.cumsum(g, axis=2)
    causal = jnp.tril(jnp.ones((CHUNK, CHUNK), dtype=bool))
    strict = jnp.tril(jnp.ones((CHUNK, CHUNK), dtype=bool), -1)
    seg = gcum[:, :, :, None, :] - gcum[:, :, None, :, :]
    gamma = jnp.exp(jnp.where(causal[:, :, None], seg, -jnp.inf))
    kk = jnp.einsum('bclhd,bcshd->bclsh', k, k)
    a_mat = jnp.where(strict[:, :, None], kk * gamma * beta[:, :, :, None, :], 0.0)
    rhs = jnp.concatenate([v * beta[..., None], k * (beta * jnp.exp(gcum))[..., None]], axis=-1)
    sol = lax.linalg.triangular_solve(a_mat.transpose(0, 1, 4, 2, 3), rhs.transpose(0, 1, 3, 2, 4),
                                      left_side=True, lower=True, unit_diagonal=True)
    u_c, w_c = sol[..., :dv], sol[..., dv:]
    attn = jnp.einsum('bclhd,bcshd->bchls', q, k) * gamma.transpose(0, 1, 4, 2, 3)
    qg = (q * jnp.exp(gcum)[..., None]).transpose(0, 1, 3, 2, 4)
    kd = (k * jnp.exp(gcum[:, :, -1:] - gcum)[..., None]).transpose(0, 1, 3, 2, 4)
    last = jnp.exp(gcum[:, :, -1])

    def step(state, inp):
        u_i, w_i, qg_i, kd_i, attn_i, last_i = inp
        v_new = u_i - jnp.einsum('bhlk,bhkv->bhlv', w_i, state)
        o = jnp.einsum('bhlk,bhkv->bhlv', qg_i, state) + jnp.einsum('bhls,bhsv->bhlv', attn_i, v_new)
        state = state * last_i[..., None, None] + jnp.einsum('bhlk,bhlv->bhkv', kd_i, v_new)
        return state, o

    init = jnp.zeros((bsz, h, dk, dv), q.dtype)
    xs = tuple(jnp.moveaxis(z, 1, 0) for z in (u_c, w_c, qg, kd, attn, last))
    _, o = lax.scan(step, init, xs)
    return o.transpose(1, 0, 3, 2, 4).reshape(bsz, t, h, dv)


def _gdn_branch(qkv, a_raw, b_raw, z, conv_w, dt_bias, a_log, norm_g):
    bsz, t, _ = qkv.shape
    qkv = jax.nn.silu(_causal_dwconv(qkv, conv_w)).astype(jnp.float32)
    q, k, v = jnp.split(qkv, 3, axis=-1)
    heads = lambda u: u.reshape(bsz, t, GDN_HEADS, GDN_HEAD_DIM)
    q = _l2norm(heads(q)) * GDN_HEAD_DIM ** -0.5
    k = _l2norm(heads(k))
    v = heads(v)
    beta = jax.nn.sigmoid(b_raw.astype(jnp.float32))
    g = -jnp.exp(a_log.astype(jnp.float32)) * jax.nn.softplus(a_raw.astype(jnp.float32) + dt_bias.astype(jnp.float32))
    o = _gated_delta_chunked(_front_pad(q), _front_pad(k), _front_pad(v), _front_pad(beta), _front_pad(g))[:, META_PAD:]
    o = _rms_norm(o, norm_g).reshape(bsz, t, GDN_WIDTH) * jax.nn.silu(z.astype(jnp.float32))
    return o.astype(z.dtype)


def _inv_softplus(y):
    return y + jnp.log(-jnp.expm1(-y))


def _log_uniform(key, shape, lo, hi):
    return jnp.exp(jax.random.uniform(key, shape, jnp.float32, math.log(lo), math.log(hi)))


def setup_inputs(seed: int = 0) -> dict:
    key = jax.random.key(seed)
    ks = jax.random.split(key, 32)

    def nrm(i, shape, scale):
        return scale * jax.random.normal(ks[i], shape, jnp.float32)

    n_idx = jnp.arange(S5_STATE, dtype=jnp.float32)
    s5_shape = (DEPTH, S5_GROUPS, S5_STATE)
    return {
        'x': nrm(0, (BATCH, SEQ, D_MODEL), 1.0),
        'meta': nrm(1, (N_META, D_MODEL), 1.0),
        'ln_in_g': 1.0 + nrm(2, (D_MODEL,), 0.02),
        'ln_in_b': nrm(3, (D_MODEL,), 0.02),
        'w_in': nrm(4, (DEPTH, D_MODEL, IN_WIDTH), D_MODEL ** -0.5),
        's5_a_re': -0.5 + nrm(5, s5_shape, 0.01),
        's5_a_im': math.pi * n_idx + nrm(6, s5_shape, 0.01),
        's5_log_step': jax.random.uniform(ks[7], (DEPTH, S5_GROUPS), jnp.float32, math.log(S5_STEP_MIN), math.log(S5_STEP_MAX)),
        's5_b_re': nrm(8, (DEPTH, S5_GROUPS, S5_STATE, S5_GROUP), S5_GROUP ** -0.5),
        's5_b_im': nrm(9, (DEPTH, S5_GROUPS, S5_STATE, S5_GROUP), S5_GROUP ** -0.5),
        's5_c_re': nrm(10, (DEPTH, S5_GROUPS, S5_GROUP, S5_STATE), S5_STATE ** -0.5),
        's5_c_im': nrm(11, (DEPTH, S5_GROUPS, S5_GROUP, S5_STATE), S5_STATE ** -0.5),
        's5_d': nrm(12, (DEPTH, S5_WIDTH), 1.0),
        's5_w_glu': nrm(13, (DEPTH, S5_WIDTH, S5_WIDTH), S5_WIDTH ** -0.5),
        's5_b_glu': nrm(14, (DEPTH, S5_WIDTH), 0.02),
        'ssd_conv_w': nrm(15, (DEPTH, CONV_K, SSD_CONV_WIDTH), CONV_K ** -0.5),
        'ssd_conv_b': nrm(16, (DEPTH, SSD_CONV_WIDTH), 0.02),
        'ssd_dt_bias': _inv_softplus(_log_uniform(ks[17], (DEPTH, SSD_HEADS), 1e-3, 1e-1)),
        'ssd_a_log': jnp.log(jax.random.uniform(ks[18], (DEPTH, SSD_HEADS), jnp.float32, 1.0, 16.0)),
        'ssd_d': 1.0 + nrm(19, (DEPTH, SSD_HEADS), 0.1),
        'ssd_norm_g': 1.0 + nrm(20, (DEPTH, SSD_WIDTH), 0.02),
        'gdn_conv_w': nrm(21, (DEPTH, CONV_K, 3 * GDN_WIDTH), CONV_K ** -0.5),
        'gdn_dt_bias': _inv_softplus(_log_uniform(ks[22], (DEPTH, GDN_HEADS), 1e-3, 1e-1)),
        'gdn_a_log': jnp.log(jax.random.uniform(ks[23], (DEPTH, GDN_HEADS), jnp.float32, 1.0, 16.0)),
        'gdn_norm_g': 1.0 + nrm(24, (DEPTH, GDN_HEAD_DIM), 0.02),
        'w_branch': nrm(25, (DEPTH, N_BRANCH, BRANCH_WIDTH, D_MODEL), BRANCH_WIDTH ** -0.5 * BETA),
        'b_gate': nrm(26, (DEPTH, N_BRANCH, D_MODEL), 0.02),
        'w_out': nrm(27, (DEPTH, D_MODEL, D_MODEL), D_MODEL ** -0.5 * BETA),
        'ln_g': 1.0 + nrm(28, (DEPTH, D_MODEL), 0.02),
        'ln_b': nrm(29, (DEPTH, D_MODEL), 0.02),
    }


def reference(x, meta, ln_in_g, ln_in_b, w_in, s5_a_re, s5_a_im, s5_log_step, s5_b_re, s5_b_im,
              s5_c_re, s5_c_im, s5_d, s5_w_glu, s5_b_glu, ssd_conv_w, ssd_conv_b, ssd_dt_bias,
              ssd_a_log, ssd_d, ssd_norm_g, gdn_conv_w, gdn_dt_bias, gdn_a_log, gdn_norm_g,
              w_branch, b_gate, w_out, ln_g, ln_b):
    bsz = x.shape[0]
    h = jnp.concatenate([jnp.broadcast_to(meta[None].astype(x.dtype), (bsz, N_META, D_MODEL)), x], axis=1)
    h = _layer_norm(h, ln_in_g, ln_in_b)
    t = h.shape[1]
    points = _split_points()
    for layer in range(DEPTH):
        proj = h @ w_in[layer]
        (s5_u, s5_z, ssd_xbc, ssd_dt, ssd_z, gdn_qkv, gdn_a, gdn_b, gdn_z,
         gate_logits) = jnp.split(proj, points, axis=-1)
        y_a = _s5_branch(s5_u, s5_z, s5_a_re[layer], s5_a_im[layer], s5_log_step[layer], s5_b_re[layer],
                         s5_b_im[layer], s5_c_re[layer], s5_c_im[layer], s5_d[layer], s5_w_glu[layer],
                         s5_b_glu[layer])
        y_b = _ssd_branch(ssd_xbc, ssd_dt, ssd_z, ssd_conv_w[layer], ssd_conv_b[layer], ssd_dt_bias[layer],
                          ssd_a_log[layer], ssd_d[layer], ssd_norm_g[layer])
        y_c = _gdn_branch(gdn_qkv, gdn_a, gdn_b, gdn_z, gdn_conv_w[layer], gdn_dt_bias[layer],
                          gdn_a_log[layer], gdn_norm_g[layer])
        branches = jnp.stack([y_a, y_b, y_c], axis=2)
        outs = jnp.einsum('btnw,nwd->btnd', branches, w_branch[layer])
        gates = jax.nn.sigmoid(gate_logits.reshape(bsz, t, N_BRANCH, D_MODEL) + b_gate[layer])
        merged = jnp.sum(gates * outs, axis=2)
        h = _layer_norm(ALPHA * h + merged @ w_out[layer], ln_g[layer], ln_b[layer])
    return h[:, N_META:]
```

```python
import functools
import math

import jax
import jax.numpy as jnp
from jax import lax
from jax.experimental import pallas as pl
from jax.experimental.pallas import tpu as pltpu

D_MODEL = 1024
N_META = 16
CONV_K = 4
N_BRANCH = 3
BRANCH_WIDTH = 768

S5_GROUP = 16
S5_GROUPS = BRANCH_WIDTH // S5_GROUP
S5_STATE = 64
S5_STATES = S5_GROUPS * S5_STATE
S5_SLAB_IN = 256
S5_SLAB_ST = 1024
S5_SLABS = BRANCH_WIDTH // S5_SLAB_IN
S5_T = 64

SSD_HEAD_DIM = 64
SSD_HEADS = 12
SSD_GROUPS = 2
SSD_HPG = SSD_HEADS // SSD_GROUPS
SSD_STATE = 128
SSD_BC = SSD_GROUPS * SSD_STATE
SSD_CONV_WIDTH = BRANCH_WIDTH + 2 * SSD_BC

GDN_HEAD_DIM = 128
GDN_HEADS = 6
GDN_QKV = 3 * BRANCH_WIDTH

CHUNK = 128
LANES = 128
SUBLANES = 8
LN_EPS = 1e-5

SM_SSD_DT = 0
SM_GDN_A = SSD_HEADS
SM_GDN_B = SSD_HEADS + GDN_HEADS

VMEM_LIMIT = 56 * 1024 * 1024

F32 = jnp.float32
BF16 = jnp.bfloat16
HIGHEST = lax.Precision.HIGHEST


def _dot(a, b):
    return jnp.dot(a, b, preferred_element_type=F32)


def _dot_f32(a, b):
    return jnp.dot(a, b, preferred_element_type=F32, precision=HIGHEST)


def _dot_nt(a, b):
    return lax.dot_general(a, b, (((1,), (1,)), ((), ())), preferred_element_type=F32)


def _sigmoid(x):
    return 1.0 / (1.0 + jnp.exp(-x))


def _silu(x):
    return x * _sigmoid(x)


def _softplus(x):
    return jnp.maximum(x, 0.0) + jnp.log1p(jnp.exp(-jnp.abs(x)))


def _gelu_tanh(x):
    c = math.sqrt(2.0 / math.pi)
    return 0.5 * x * (1.0 + jnp.tanh(c * (x + 0.044715 * (x * x * x))))


def _pick(n, candidates):
    for c in candidates:
        if n % c == 0:
            return c
    raise ValueError(f"no tile for {n} in {candidates}")


def _params(*semantics):
    return pltpu.CompilerParams(dimension_semantics=semantics, vmem_limit_bytes=VMEM_LIMIT)


def _full(shape):
    n = len(shape)
    return pl.BlockSpec(shape, lambda *_: (0,) * n)


def _ln_math(x, g, b):
    mu = jnp.mean(x, axis=-1, keepdims=True)
    xc = x - mu
    var = jnp.mean(xc * xc, axis=-1, keepdims=True)
    return xc * lax.rsqrt(var + LN_EPS) * g + b


def _ln_kernel(x_ref, g_ref, b_ref, o_ref, ob_ref):
    y = _ln_math(x_ref[...], g_ref[...], b_ref[...])
    o_ref[...] = y
    ob_ref[...] = y.astype(BF16)


def _layer_norm_in(x2d, g, b):
    m, d = x2d.shape
    tm = _pick(m, (512, 256, 128))
    return pl.pallas_call(
        _ln_kernel,
        grid=(m // tm,),
        in_specs=[pl.BlockSpec((tm, d), lambda i: (i, 0)), _full((1, d)), _full((1, d))],
        out_specs=[pl.BlockSpec((tm, d), lambda i: (i, 0)), pl.BlockSpec((tm, d), lambda i: (i, 0))],
        out_shape=[jax.ShapeDtypeStruct((m, d), F32), jax.ShapeDtypeStruct((m, d), BF16)],
        compiler_params=_params("parallel"),
        name="ln_in",
    )(x2d, g.reshape(1, d), b.reshape(1, d))


def _mm_kernel(x_ref, w_ref, o_ref):
    o_ref[...] = _dot(x_ref[...], w_ref[...]).astype(o_ref.dtype)


def _matmul(x, w, out_dtype, name):
    m, k = x.shape
    n = w.shape[1]
    tm = _pick(m, (1536, 1024, 512, 256, 128))
    return pl.pallas_call(
        _mm_kernel,
        grid=(m // tm,),
        in_specs=[pl.BlockSpec((tm, k), lambda i: (i, 0)), _full((k, n))],
        out_specs=pl.BlockSpec((tm, n), lambda i: (i, 0)),
        out_shape=jax.ShapeDtypeStruct((m, n), out_dtype),
        compiler_params=_params("parallel"),
        name=name,
    )(x, w)


def _s5_kernel(uz_ref, wbre_ref, wbim_ref, pnr_ref, pni_ref, ppr_ref, ppi_ref, pcr_ref, pci_ref,
               cre_ref, cim_ref, d_ref, wglu_ref, bglu_ref, out_ref,
               bure, buim, sre, sim, st_re, st_im, *, rb):
    @pl.when(pl.program_id(1) == 0)
    def _():
        st_re[...] = jnp.zeros_like(st_re)
        st_im[...] = jnp.zeros_like(st_im)

    u = uz_ref[:, :BRANCH_WIDTH]
    for k in range(S5_SLABS):
        uk = u[:, k * S5_SLAB_IN:(k + 1) * S5_SLAB_IN]
        sl = slice(k * S5_SLAB_ST, (k + 1) * S5_SLAB_ST)
        bure[:, sl] = _dot(uk, wbre_ref[k])
        buim[:, sl] = _dot(uk, wbim_ref[k])

    t = S5_T
    row = lax.broadcasted_iota(jnp.int32, (t, t), 0)
    col = lax.broadcasted_iota(jnp.int32, (t, t), 1)
    tril = jnp.where(row >= col, 1.0, 0.0).astype(BF16)

    def body(c, carry):
        r0 = pl.multiple_of(c * t, t)
        for k in range(S5_SLABS):
            sl = slice(k * S5_SLAB_ST, (k + 1) * S5_SLAB_ST)
            br = bure[pl.ds(r0, t), sl]
            bi = buim[pl.ds(r0, t), sl]
            pnr = pnr_ref[:, sl]
            pni = pni_ref[:, sl]
            xr = (br * pnr - bi * pni).astype(BF16)
            xi = (br * pni + bi * pnr).astype(BF16)
            acc_r = _dot(tril, xr)
            acc_i = _dot(tril, xi)
            ppr = ppr_ref[:, sl]
            ppi = ppi_ref[:, sl]
            pcr = pcr_ref[:, sl]
            pci = pci_ref[:, sl]
            s_in_r = st_re[:, sl]
            s_in_i = st_im[:, sl]
            s_r = acc_r * ppr - acc_i * ppi + (pcr * s_in_r - pci * s_in_i)
            s_i = acc_r * ppi + acc_i * ppr + (pcr * s_in_i + pci * s_in_r)
            sre[pl.ds(r0, t), sl] = s_r.astype(BF16)
            sim[pl.ds(r0, t), sl] = s_i.astype(BF16)
            st_re[:, sl] = s_r[t - 1:t, :]
            st_im[:, sl] = s_i[t - 1:t, :]
        return carry

    lax.fori_loop(0, rb // t, body, 0)

    ys = []
    for k in range(S5_SLABS):
        sl = slice(k * S5_SLAB_ST, (k + 1) * S5_SLAB_ST)
        ys.append(_dot(sre[:, sl], cre_ref[k]) - _dot(sim[:, sl], cim_ref[k]))
    y = jnp.concatenate(ys, axis=-1) + d_ref[...] * u.astype(F32)
    v = _gelu_tanh(y)
    v = v * _sigmoid(_dot(v.astype(BF16), wglu_ref[...]) + bglu_ref[...])
    z = uz_ref[:, BRANCH_WIDTH:].astype(F32)
    out_ref[...] = (v * _silu(z)).astype(out_ref.dtype)


def _s5_branch(uz, prm, bsz, lp):
    rb = _pick(lp, (384, 256, 128))
    nblk = lp // rb
    w2 = 2 * BRANCH_WIDTH
    tab = pl.BlockSpec((S5_T, S5_STATES), lambda b, r: (0, 0))
    wb = pl.BlockSpec((S5_SLABS, S5_SLAB_IN, S5_SLAB_ST), lambda b, r: (0, 0, 0))
    cb = pl.BlockSpec((S5_SLABS, S5_SLAB_ST, S5_SLAB_IN), lambda b, r: (0, 0, 0))
    vec = pl.BlockSpec((1, BRANCH_WIDTH), lambda b, r: (0, 0))
    return pl.pallas_call(
        functools.partial(_s5_kernel, rb=rb),
        grid=(bsz, nblk),
        in_specs=[pl.BlockSpec((rb, w2), lambda b, r: (b * nblk + r, 0)),
                  wb, wb, tab, tab, tab, tab, tab, tab, cb, cb, vec,
                  pl.BlockSpec((BRANCH_WIDTH, BRANCH_WIDTH), lambda b, r: (0, 0)), vec],
        out_specs=pl.BlockSpec((rb, BRANCH_WIDTH), lambda b, r: (b * nblk + r, 0)),
        out_shape=jax.ShapeDtypeStruct((bsz * lp, BRANCH_WIDTH), BF16),
        scratch_shapes=[pltpu.VMEM((rb, S5_STATES), F32), pltpu.VMEM((rb, S5_STATES), F32),
                        pltpu.VMEM((rb, S5_STATES), BF16), pltpu.VMEM((rb, S5_STATES), BF16),
                        pltpu.VMEM((1, S5_STATES), F32), pltpu.VMEM((1, S5_STATES), F32)],
        compiler_params=_params("parallel", "arbitrary"),
        name="s5_branch",
    )(uz, prm["wb_re"], prm["wb_im"], prm["pn_re"], prm["pn_im"], prm["pp_re"], prm["pp_im"],
      prm["pc_re"], prm["pc_im"], prm["c_re"], prm["c_im"], prm["d"], prm["w_glu"], prm["b_glu"])


def _s5_prepare(a_re, a_im, log_step, b_re, b_im, c_re, c_im, d, w_glu, b_glu):
    lam_re = jnp.minimum(a_re, -1e-4)
    lam_im = a_im
    step = jnp.exp(log_step)[:, None]
    mag = jnp.exp(lam_re * step)
    abar_re, abar_im = mag * jnp.cos(lam_im * step), mag * jnp.sin(lam_im * step)
    den = lam_re * lam_re + lam_im * lam_im
    nr, ni = abar_re - 1.0, abar_im
    coef_re = (nr * lam_re + ni * lam_im) / den
    coef_im = (ni * lam_re - nr * lam_im) / den
    bbar_re = coef_re[..., None] * b_re - coef_im[..., None] * b_im
    bbar_im = coef_re[..., None] * b_im + coef_im[..., None] * b_re

    gps = S5_SLAB_IN // S5_GROUP
    eye = jnp.eye(gps, dtype=F32)

    def in_slabs(bb):
        w = jnp.einsum('kgpc,gh->kgchp', bb.reshape(S5_SLABS, gps, S5_STATE, S5_GROUP), eye)
        return w.reshape(S5_SLABS, S5_SLAB_IN, S5_SLAB_ST).astype(BF16)

    def out_slabs(cc):
        w = jnp.einsum('kgcp,gh->kgphc', cc.reshape(S5_SLABS, gps, S5_GROUP, S5_STATE), eye)
        return w.reshape(S5_SLABS, S5_SLAB_ST, S5_SLAB_IN).astype(BF16)

    dec = (lam_re * step).reshape(1, S5_STATES)
    ang = (lam_im * step).reshape(1, S5_STATES)

    def power(n):
        m = jnp.exp(n * dec)
        return m * jnp.cos(n * ang), m * jnp.sin(n * ang)

    idx = jnp.arange(S5_T, dtype=F32)[:, None]
    pn_re, pn_im = power(-idx)
    pp_re, pp_im = power(idx)
    pc_re, pc_im = power(idx + 1.0)
    return dict(wb_re=in_slabs(bbar_re), wb_im=in_slabs(bbar_im),
                pn_re=pn_re, pn_im=pn_im, pp_re=pp_re, pp_im=pp_im, pc_re=pc_re, pc_im=pc_im,
                c_re=out_slabs(c_re), c_im=out_slabs(c_im), d=d.reshape(1, -1),
                w_glu=w_glu.astype(BF16), b_glu=b_glu.reshape(1, -1))


def _causal_conv(buf, x, w_ref, rows):
    buf[pl.ds(SUBLANES, rows), :] = x
    acc = None
    for j in range(CONV_K):
        tap = buf[pl.ds(SUBLANES - (CONV_K - 1) + j, rows), :] * w_ref[j:j + 1, :]
        acc = tap if acc is None else acc + tap
    buf[pl.ds(0, SUBLANES), :] = buf[pl.ds(rows, SUBLANES), :]
    return acc


def _tri_masks(q):
    row = lax.broadcasted_iota(jnp.int32, (q, q), 0)
    col = lax.broadcasted_iota(jnp.int32, (q, q), 1)
    return row >= col, row > col, row == col


def _ssd_kernel(xz_ref, sm_ref, cw_ref, cb_ref, dtb_ref, nega_ref, d_ref, ng_ref, out_ref,
                buf, state, *, q):
    @pl.when(pl.program_id(1) == 0)
    def _():
        buf[pl.ds(0, SUBLANES), :] = jnp.zeros((SUBLANES, SSD_CONV_WIDTH), F32)
        state[...] = jnp.zeros_like(state)

    xbc = xz_ref[:, :SSD_CONV_WIDTH].astype(F32)
    xbc = _silu(_causal_conv(buf, xbc, cw_ref, q) + cb_ref[...])
    xs = xbc[:, :BRANCH_WIDTH]
    bmat = xbc[:, BRANCH_WIDTH:BRANCH_WIDTH + SSD_BC]
    cmat = xbc[:, BRANCH_WIDTH + SSD_BC:]

    causal, _, _ = _tri_masks(q)
    tril = jnp.where(causal, 1.0, 0.0)
    dt = _softplus(sm_ref[...] + dtb_ref[...])
    da = dt * nega_ref[...]
    acum = _dot_f32(tril, da)
    acum_t = acum.T
    alast = acum[q - 1:q, :]
    e_in = jnp.exp(acum)
    e_out = jnp.exp(alast - acum)
    e_all = jnp.exp(alast)

    ys = []
    for g in range(SSD_GROUPS):
        bg = bmat[:, g * SSD_STATE:(g + 1) * SSD_STATE].astype(BF16)
        cg = cmat[:, g * SSD_STATE:(g + 1) * SSD_STATE].astype(BF16)
        scores = _dot_nt(cg, bg)
        st = state[g]
        cs = _dot(cg, st.astype(BF16))
        xdw, dec = [], []
        for hh in range(SSD_HPG):
            h = g * SSD_HPG + hh
            col = slice(h * SSD_HEAD_DIM, (h + 1) * SSD_HEAD_DIM)
            xh = xs[:, col]
            xd = xh * dt[:, h:h + 1]
            lmat = jnp.where(causal, jnp.exp(acum[:, h:h + 1] - acum_t[h:h + 1, :]), 0.0)
            y = _dot((scores * lmat).astype(BF16), xd.astype(BF16))
            y = y + cs[:, hh * SSD_HEAD_DIM:(hh + 1) * SSD_HEAD_DIM] * e_in[:, h:h + 1]
            ys.append(y + xh * d_ref[:, col])
            xdw.append(xd * e_out[:, h:h + 1])
            dec.append(jnp.broadcast_to(e_all[:, h:h + 1], (1, SSD_HEAD_DIM)))
        xdw = jnp.concatenate(xdw, axis=-1).astype(BF16)
        dec = jnp.concatenate(dec, axis=-1)
        state[g] = st * dec + _dot(bmat[:, g * SSD_STATE:(g + 1) * SSD_STATE].T.astype(BF16), xdw)

    y = jnp.concatenate(ys, axis=-1) * _silu(xz_ref[:, SSD_CONV_WIDTH:].astype(F32))
    y = y * lax.rsqrt(jnp.mean(y * y, axis=-1, keepdims=True) + LN_EPS) * ng_ref[...]
    out_ref[...] = y.astype(out_ref.dtype)


def _ssd_branch(xz, small, prm, bsz, lp):
    q = CHUNK
    nblk = lp // q
    wx = SSD_CONV_WIDTH + BRANCH_WIDTH
    rowmap = lambda b, r: (b * nblk + r, 0)
    c2 = lambda b, r: (0, 0)
    return pl.pallas_call(
        functools.partial(_ssd_kernel, q=q),
        grid=(bsz, nblk),
        in_specs=[pl.BlockSpec((q, wx), rowmap), pl.BlockSpec((q, LANES), rowmap),
                  pl.BlockSpec((CONV_K, SSD_CONV_WIDTH), c2), pl.BlockSpec((1, SSD_CONV_WIDTH), c2),
                  pl.BlockSpec((1, LANES), c2), pl.BlockSpec((1, LANES), c2),
                  pl.BlockSpec((1, BRANCH_WIDTH), c2), pl.BlockSpec((1, BRANCH_WIDTH), c2)],
        out_specs=pl.BlockSpec((q, BRANCH_WIDTH), rowmap),
        out_shape=jax.ShapeDtypeStruct((bsz * lp, BRANCH_WIDTH), BF16),
        scratch_shapes=[pltpu.VMEM((q + SUBLANES, SSD_CONV_WIDTH), F32),
                        pltpu.VMEM((SSD_GROUPS, SSD_STATE, SSD_HPG * SSD_HEAD_DIM), F32)],
        compiler_params=_params("parallel", "arbitrary"),
        name="ssd_branch",
    )(xz, small, prm["conv_w"], prm["conv_b"], prm["dt_bias"], prm["neg_a"], prm["d"], prm["norm_g"])


def _lane_vector(values, offset):
    n = values.shape[0]
    return jnp.pad(values.astype(F32), (offset, LANES - offset - n)).reshape(1, LANES)


def _ssd_prepare(conv_w, conv_b, dt_bias, a_log, d, norm_g):
    return dict(conv_w=conv_w, conv_b=conv_b.reshape(1, -1),
                dt_bias=_lane_vector(dt_bias, SM_SSD_DT),
                neg_a=_lane_vector(-jnp.exp(a_log), SM_SSD_DT),
                d=jnp.repeat(d, SSD_HEAD_DIM).reshape(1, -1), norm_g=norm_g.reshape(1, -1))


def _gdn_kernel(qkvz_ref, sm_ref, cw_ref, dtb_ref, nega_ref, ng_ref, out_ref, buf, state, *, q):
    @pl.when(pl.program_id(1) == 0)
    def _():
        buf[pl.ds(0, SUBLANES), :] = jnp.zeros((SUBLANES, GDN_QKV), F32)
        state[...] = jnp.zeros_like(state)

    qkv = _silu(_causal_conv(buf, qkvz_ref[:, :GDN_QKV].astype(F32), cw_ref, q))

    causal, strict, diag = _tri_masks(q)
    tril = jnp.where(causal, 1.0, 0.0)
    eye = jnp.where(diag, 1.0, 0.0)
    sm = sm_ref[...]
    glog = nega_ref[...] * _softplus(sm + dtb_ref[...])
    gcum = _dot_f32(tril, glog)
    gcum_t = gcum.T
    beta_all = _sigmoid(sm)
    e_in = jnp.exp(gcum)
    glast = gcum[q - 1:q, :]
    e_out = jnp.exp(glast - gcum)
    e_all = jnp.exp(glast)

    outs = []
    for h in range(GDN_HEADS):
        la = SM_GDN_A + h
        lb = SM_GDN_B + h
        col = slice(h * GDN_HEAD_DIM, (h + 1) * GDN_HEAD_DIM)
        qh = qkv[:, col]
        kh = qkv[:, BRANCH_WIDTH + h * GDN_HEAD_DIM:BRANCH_WIDTH + (h + 1) * GDN_HEAD_DIM]
        vh = qkv[:, 2 * BRANCH_WIDTH + h * GDN_HEAD_DIM:2 * BRANCH_WIDTH + (h + 1) * GDN_HEAD_DIM]
        qh = qh * lax.rsqrt(jnp.sum(qh * qh, axis=-1, keepdims=True) + 1e-6) * (GDN_HEAD_DIM ** -0.5)
        kh = kh * lax.rsqrt(jnp.sum(kh * kh, axis=-1, keepdims=True) + 1e-6)
        beta = beta_all[:, lb:lb + 1]
        gamma = jnp.where(causal, jnp.exp(gcum[:, la:la + 1] - gcum_t[la:la + 1, :]), 0.0)
        kb = kh.astype(BF16)
        kk = _dot_nt(kb, kb)
        pw = jnp.where(strict, -(kk * gamma * beta), 0.0)
        tinv = eye + pw
        steps = int(math.log2(q)) - 1
        for _ in range(steps):
            pw = _dot_f32(pw, pw)
            tinv = tinv + _dot_f32(tinv, pw)
        rhs = jnp.concatenate([vh * beta, kh * (beta * e_in[:, la:la + 1])], axis=-1)
        uw = _dot_f32(tinv, rhs)
        u_c = uw[:, :GDN_HEAD_DIM]
        w_c = uw[:, GDN_HEAD_DIM:]
        attn = _dot_nt(qh.astype(BF16), kb) * gamma
        st = state[h]
        stb = st.astype(BF16)
        v_new = u_c - _dot(w_c.astype(BF16), stb)
        vb = v_new.astype(BF16)
        o = _dot((qh * e_in[:, la:la + 1]).astype(BF16), stb) + _dot(attn.astype(BF16), vb)
        kd = kh * e_out[:, la:la + 1]
        state[h] = st * e_all[:, la:la + 1] + _dot(kd.T.astype(BF16), vb)
        o = o * lax.rsqrt(jnp.mean(o * o, axis=-1, keepdims=True) + LN_EPS) * ng_ref[...]
        outs.append(o)

    o = jnp.concatenate(outs, axis=-1) * _silu(qkvz_ref[:, GDN_QKV:].astype(F32))
    out_ref[...] = o.astype(out_ref.dtype)


def _gdn_branch(qkvz, small, prm, bsz, lp):
    q = CHUNK
    nblk = lp // q
    wx = GDN_QKV + BRANCH_WIDTH
    rowmap = lambda b, r: (b * nblk + r, 0)
    c2 = lambda b, r: (0, 0)
    return pl.pallas_call(
        functools.partial(_gdn_kernel, q=q),
        grid=(bsz, nblk),
        in_specs=[pl.BlockSpec((q, wx), rowmap), pl.BlockSpec((q, LANES), rowmap),
                  pl.BlockSpec((CONV_K, GDN_QKV), c2),
                  pl.BlockSpec((1, LANES), c2), pl.BlockSpec((1, LANES), c2),
                  pl.BlockSpec((1, GDN_HEAD_DIM), c2)],
        out_specs=pl.BlockSpec((q, BRANCH_WIDTH), rowmap),
        out_shape=jax.ShapeDtypeStruct((bsz * lp, BRANCH_WIDTH), BF16),
        scratch_shapes=[pltpu.VMEM((q + SUBLANES, GDN_QKV), F32),
                        pltpu.VMEM((GDN_HEADS, GDN_HEAD_DIM, GDN_HEAD_DIM), F32)],
        compiler_params=_params("parallel", "arbitrary"),
        name="gdn_branch",
    )(qkvz, small, prm["conv_w"], prm["dt_bias"], prm["neg_a"], prm["norm_g"])


def _gdn_prepare(conv_w, dt_bias, a_log, norm_g):
    return dict(conv_w=conv_w, dt_bias=_lane_vector(dt_bias, SM_GDN_A),
                neg_a=_lane_vector(-jnp.exp(a_log), SM_GDN_A), norm_g=norm_g.reshape(1, -1))


def _merge_kernel(h_ref, ya_ref, yb_ref, yc_ref, wg_ref, bg_ref, wbr_ref, wo_ref, g_ref, b_ref,
                  o_ref, ob_ref, *, alpha):
    h = h_ref[...]
    logits = _dot(h.astype(BF16), wg_ref[...]) + bg_ref[...]
    merged = None
    for n, y_ref in enumerate((ya_ref, yb_ref, yc_ref)):
        gate = _sigmoid(logits[:, n * D_MODEL:(n + 1) * D_MODEL])
        term = gate * _dot(y_ref[...], wbr_ref[n])
        merged = term if merged is None else merged + term
    y = alpha * h + _dot(merged.astype(BF16), wo_ref[...])
    y = _ln_math(y, g_ref[...], b_ref[...])
    o_ref[...] = y
    ob_ref[...] = y.astype(BF16)


def _merge(h, ya, yb, yc, prm, alpha):
    m, d = h.shape
    tm = _pick(m, (512, 256, 128))
    rows = lambda w: pl.BlockSpec((tm, w), lambda i: (i, 0))
    return pl.pallas_call(
        functools.partial(_merge_kernel, alpha=alpha),
        grid=(m // tm,),
        in_specs=[rows(d), rows(BRANCH_WIDTH), rows(BRANCH_WIDTH), rows(BRANCH_WIDTH),
                  _full((d, N_BRANCH * d)), _full((1, N_BRANCH * d)),
                  _full((N_BRANCH, BRANCH_WIDTH, d)), _full((d, d)), _full((1, d)), _full((1, d))],
        out_specs=[rows(d), rows(d)],
        out_shape=[jax.ShapeDtypeStruct((m, d), F32), jax.ShapeDtypeStruct((m, d), BF16)],
        compiler_params=_params("parallel"),
        name="merge",
    )(h, ya, yb, yc, prm["w_gate"], prm["b_gate"], prm["w_branch"], prm["w_out"], prm["ln_g"], prm["ln_b"])


def _split_in_proj(w):
    widths = (BRANCH_WIDTH, BRANCH_WIDTH, SSD_CONV_WIDTH, SSD_HEADS, BRANCH_WIDTH,
              GDN_QKV, GDN_HEADS, GDN_HEADS, BRANCH_WIDTH, N_BRANCH * D_MODEL)
    offs = [0]
    for wd in widths:
        offs.append(offs[-1] + wd)
    seg = [w[:, offs[i]:offs[i + 1]] for i in range(len(widths))]
    s5_u, s5_z, ssd_xbc, ssd_dt, ssd_z, gdn_qkv, gdn_a, gdn_b, gdn_z, gate = seg
    pad = jnp.zeros((w.shape[0], LANES - SSD_HEADS - 2 * GDN_HEADS), w.dtype)
    return dict(
        s5=jnp.concatenate([s5_u, s5_z], axis=1).astype(BF16),
        ssd=jnp.concatenate([ssd_xbc, ssd_z], axis=1).astype(BF16),
        gdn=jnp.concatenate([gdn_qkv, gdn_z], axis=1).astype(BF16),
        small=jnp.concatenate([ssd_dt, gdn_a, gdn_b, pad], axis=1).astype(BF16),
        gate=gate.astype(BF16))


def kernel(x, meta, ln_in_g, ln_in_b, w_in, s5_a_re, s5_a_im, s5_log_step, s5_b_re, s5_b_im, s5_c_re, s5_c_im, s5_d, s5_w_glu, s5_b_glu, ssd_conv_w, ssd_conv_b, ssd_dt_bias, ssd_a_log, ssd_d, ssd_norm_g, gdn_conv_w, gdn_dt_bias, gdn_a_log, gdn_norm_g, w_branch, b_gate, w_out, ln_g, ln_b):
    bsz, seq, d = x.shape
    depth = w_in.shape[0]
    alpha = (2 * depth) ** 0.25
    lr = N_META + seq
    lp = -(-lr // CHUNK) * CHUNK
    h0 = jnp.concatenate([jnp.broadcast_to(meta[None].astype(x.dtype), (bsz, N_META, d)), x,
                          jnp.zeros((bsz, lp - lr, d), x.dtype)], axis=1).reshape(bsz * lp, d)
    h, hb = _layer_norm_in(h0, ln_in_g, ln_in_b)

    for layer in range(depth):
        wi = _split_in_proj(w_in[layer])
        s5p = _s5_prepare(s5_a_re[layer], s5_a_im[layer], s5_log_step[layer], s5_b_re[layer], s5_b_im[layer],
                          s5_c_re[layer], s5_c_im[layer], s5_d[layer], s5_w_glu[layer], s5_b_glu[layer])
        ssdp = _ssd_prepare(ssd_conv_w[layer], ssd_conv_b[layer], ssd_dt_bias[layer], ssd_a_log[layer],
                            ssd_d[layer], ssd_norm_g[layer])
        gdnp = _gdn_prepare(gdn_conv_w[layer], gdn_dt_bias[layer], gdn_a_log[layer], gdn_norm_g[layer])
        mrg = dict(w_gate=wi["gate"], b_gate=b_gate[layer].reshape(1, -1),
                   w_branch=w_branch[layer].astype(BF16), w_out=w_out[layer].astype(BF16),
                   ln_g=ln_g[layer].reshape(1, -1), ln_b=ln_b[layer].reshape(1, -1))

        small = _matmul(hb, wi["small"], F32, "proj_small")
        y_a = _s5_branch(_matmul(hb, wi["s5"], BF16, "proj_s5"), s5p, bsz, lp)
        y_b = _ssd_branch(_matmul(hb, wi["ssd"], BF16, "proj_ssd"), small, ssdp, bsz, lp)
        y_c = _gdn_branch(_matmul(hb, wi["gdn"], BF16, "proj_gdn"), small, gdnp, bsz, lp)
        h, hb = _merge(h, y_a, y_b, y_c, mrg, alpha)

    return h.reshape(bsz, lp, d)[:, N_META:lr]
```

```python
import functools
import math

import jax
import jax.numpy as jnp
from jax import lax
from jax.experimental import pallas as pl
from jax.experimental.pallas import tpu as pltpu

D_MODEL = 1024
N_META = 16
CONV_K = 4
N_BRANCH = 3
BRANCH_WIDTH = 768

S5_GROUP = 16
S5_GROUPS = BRANCH_WIDTH // S5_GROUP
S5_STATE = 64
S5_STATES = S5_GROUPS * S5_STATE
S5_SLAB_IN = 256
S5_SLAB_ST = 1024
S5_SLABS = BRANCH_WIDTH // S5_SLAB_IN
S5_T = 64

SSD_HEAD_DIM = 64
SSD_HEADS = 12
SSD_GROUPS = 2
SSD_HPG = SSD_HEADS // SSD_GROUPS
SSD_STATE = 128
SSD_BC = SSD_GROUPS * SSD_STATE
SSD_CONV_WIDTH = BRANCH_WIDTH + 2 * SSD_BC

GDN_HEAD_DIM = 128
GDN_HEADS = 6
GDN_QKV = 3 * BRANCH_WIDTH

CHUNK = 128
LANES = 128
SUBLANES = 8
LN_EPS = 1e-5

SM_SSD_DT = 0
SM_GDN_A = SSD_HEADS
SM_GDN_B = SSD_HEADS + GDN_HEADS

VMEM_LIMIT = 56 * 1024 * 1024

F32 = jnp.float32
BF16 = jnp.bfloat16
HIGHEST = lax.Precision.HIGHEST


def _dot(a, b):
    return jnp.dot(a, b, preferred_element_type=F32)


def _dot_f32(a, b):
    return jnp.dot(a, b, preferred_element_type=F32, precision=HIGHEST)


def _dot_nt(a, b):
    return lax.dot_general(a, b, (((1,), (1,)), ((), ())), preferred_element_type=F32)


def _sigmoid(x):
    return 1.0 / (1.0 + jnp.exp(-x))


def _silu(x):
    return x * _sigmoid(x)


def _softplus(x):
    return jnp.maximum(x, 0.0) + jnp.log1p(jnp.exp(-jnp.abs(x)))


def _gelu_tanh(x):
    c = math.sqrt(2.0 / math.pi)
    return 0.5 * x * (1.0 + jnp.tanh(c * (x + 0.044715 * (x * x * x))))


def _pick(n, candidates):
    for c in candidates:
        if n % c == 0:
            return c
    raise ValueError(f"no tile for {n} in {candidates}")


def _params(*semantics):
    return pltpu.CompilerParams(dimension_semantics=semantics, vmem_limit_bytes=VMEM_LIMIT)


def _full(shape):
    n = len(shape)
    return pl.BlockSpec(shape, lambda *_: (0,) * n)


def _ln_math(x, g, b):
    mu = jnp.mean(x, axis=-1, keepdims=True)
    xc = x - mu
    var = jnp.mean(xc * xc, axis=-1, keepdims=True)
    return xc * lax.rsqrt(var + LN_EPS) * g + b


def _ln_kernel(x_ref, g_ref, b_ref, o_ref, ob_ref):
    y = _ln_math(x_ref[...], g_ref[...], b_ref[...])
    o_ref[...] = y
    ob_ref[...] = y.astype(BF16)


def _layer_norm_in(x2d, g, b):
    m, d = x2d.shape
    tm = _pick(m, (512, 256, 128))
    return pl.pallas_call(
        _ln_kernel,
        grid=(m // tm,),
        in_specs=[pl.BlockSpec((tm, d), lambda i: (i, 0)), _full((1, d)), _full((1, d))],
        out_specs=[pl.BlockSpec((tm, d), lambda i: (i, 0)), pl.BlockSpec((tm, d), lambda i: (i, 0))],
        out_shape=[jax.ShapeDtypeStruct((m, d), F32), jax.ShapeDtypeStruct((m, d), BF16)],
        compiler_params=_params("parallel"),
        name="ln_in",
    )(x2d, g.reshape(1, d), b.reshape(1, d))


def _mm_kernel(x_ref, w_ref, o_ref):
    o_ref[...] = _dot(x_ref[...], w_ref[...]).astype(o_ref.dtype)


def _matmul(x, w, out_dtype, name):
    m, k = x.shape
    n = w.shape[1]
    tm = _pick(m, (1536, 1024, 512, 256, 128))
    return pl.pallas_call(
        _mm_kernel,
        grid=(m // tm,),
        in_specs=[pl.BlockSpec((tm, k), lambda i: (i, 0)), _full((k, n))],
        out_specs=pl.BlockSpec((tm, n), lambda i: (i, 0)),
        out_shape=jax.ShapeDtypeStruct((m, n), out_dtype),
        compiler_params=_params("parallel"),
        name=name,
    )(x, w)


def _s5_kernel(uz_ref, wbre_ref, wbim_ref, pnr_ref, pni_ref, ppr_ref, ppi_ref, pcr_ref, pci_ref,
               cre_ref, cim_ref, d_ref, wglu_ref, bglu_ref, out_ref,
               bure, buim, sre, sim, st_re, st_im, *, rb):
    @pl.when(pl.program_id(1) == 0)
    def _():
        st_re[...] = jnp.zeros_like(st_re)
        st_im[...] = jnp.zeros_like(st_im)

    u = uz_ref[:, :BRANCH_WIDTH]
    for k in range(S5_SLABS):
        uk = u[:, k * S5_SLAB_IN:(k + 1) * S5_SLAB_IN]
        sl = slice(k * S5_SLAB_ST, (k + 1) * S5_SLAB_ST)
        bure[:, sl] = _dot(uk, wbre_ref[k])
        buim[:, sl] = _dot(uk, wbim_ref[k])

    t = S5_T
    row = lax.broadcasted_iota(jnp.int32, (t, t), 0)
    col = lax.broadcasted_iota(jnp.int32, (t, t), 1)
    tril = jnp.where(row >= col, 1.0, 0.0).astype(BF16)

    def body(c, carry):
        r0 = pl.multiple_of(c * t, t)
        for k in range(S5_SLABS):
            sl = slice(k * S5_SLAB_ST, (k + 1) * S5_SLAB_ST)
            br = bure[pl.ds(r0, t), sl]
            bi = buim[pl.ds(r0, t), sl]
            pnr = pnr_ref[:, sl]
            pni = pni_ref[:, sl]
            xr = (br * pnr - bi * pni).astype(BF16)
            xi = (br * pni + bi * pnr).astype(BF16)
            acc_r = _dot(tril, xr)
            acc_i = _dot(tril, xi)
            ppr = ppr_ref[:, sl]
            ppi = ppi_ref[:, sl]
            pcr = pcr_ref[:, sl]
            pci = pci_ref[:, sl]
            s_in_r = st_re[:, sl]
            s_in_i = st_im[:, sl]
            s_r = acc_r * ppr - acc_i * ppi + (pcr * s_in_r - pci * s_in_i)
            s_i = acc_r * ppi + acc_i * ppr + (pcr * s_in_i + pci * s_in_r)
            sre[pl.ds(r0, t), sl] = s_r.astype(BF16)
            sim[pl.ds(r0, t), sl] = s_i.astype(BF16)
            st_re[:, sl] = s_r[t - 1:t, :]
            st_im[:, sl] = s_i[t - 1:t, :]
        return carry

    lax.fori_loop(0, rb // t, body, 0)

    ys = []
    for k in range(S5_SLABS):
        sl = slice(k * S5_SLAB_ST, (k + 1) * S5_SLAB_ST)
        ys.append(_dot(sre[:, sl], cre_ref[k]) - _dot(sim[:, sl], cim_ref[k]))
    y = jnp.concatenate(ys, axis=-1) + d_ref[...] * u.astype(F32)
    v = _gelu_tanh(y)
    v = v * _sigmoid(_dot(v.astype(BF16), wglu_ref[...]) + bglu_ref[...])
    z = uz_ref[:, BRANCH_WIDTH:].astype(F32)
    out_ref[...] = (v * _silu(z)).astype(out_ref.dtype)


def _s5_branch(uz, prm, bsz, lp):
    rb = _pick(lp, (384, 256, 128))
    nblk = lp // rb
    w2 = 2 * BRANCH_WIDTH
    tab = pl.BlockSpec((S5_T, S5_STATES), lambda b, r: (0, 0))
    wb = pl.BlockSpec((S5_SLABS, S5_SLAB_IN, S5_SLAB_ST), lambda b, r: (0, 0, 0))
    cb = pl.BlockSpec((S5_SLABS, S5_SLAB_ST, S5_SLAB_IN), lambda b, r: (0, 0, 0))
    vec = pl.BlockSpec((1, BRANCH_WIDTH), lambda b, r: (0, 0))
    return pl.pallas_call(
        functools.partial(_s5_kernel, rb=rb),
        grid=(bsz, nblk),
        in_specs=[pl.BlockSpec((rb, w2), lambda b, r: (b * nblk + r, 0)),
                  wb, wb, tab, tab, tab, tab, tab, tab, cb, cb, vec,
                  pl.BlockSpec((BRANCH_WIDTH, BRANCH_WIDTH), lambda b, r: (0, 0)), vec],
        out_specs=pl.BlockSpec((rb, BRANCH_WIDTH), lambda b, r: (b * nblk + r, 0)),
        out_shape=jax.ShapeDtypeStruct((bsz * lp, BRANCH_WIDTH), BF16),
        scratch_shapes=[pltpu.VMEM((rb, S5_STATES), F32), pltpu.VMEM((rb, S5_STATES), F32),
                        pltpu.VMEM((rb, S5_STATES), BF16), pltpu.VMEM((rb, S5_STATES), BF16),
                        pltpu.VMEM((1, S5_STATES), F32), pltpu.VMEM((1, S5_STATES), F32)],
        compiler_params=_params("parallel", "arbitrary"),
        name="s5_branch",
    )(uz, prm["wb_re"], prm["wb_im"], prm["pn_re"], prm["pn_im"], prm["pp_re"], prm["pp_im"],
      prm["pc_re"], prm["pc_im"], prm["c_re"], prm["c_im"], prm["d"], prm["w_glu"], prm["b_glu"])


def _s5_prepare(a_re, a_im, log_step, b_re, b_im, c_re, c_im, d, w_glu, b_glu):
    lam_re = jnp.minimum(a_re, -1e-4)
    lam_im = a_im
    step = jnp.exp(log_step)[:, None]
    mag = jnp.exp(lam_re * step)
    abar_re, abar_im = mag * jnp.cos(lam_im * step), mag * jnp.sin(lam_im * step)
    den = lam_re * lam_re + lam_im * lam_im
    nr, ni = abar_re - 1.0, abar_im
    coef_re = (nr * lam_re + ni * lam_im) / den
    coef_im = (ni * lam_re - nr * lam_im) / den
    bbar_re = coef_re[..., None] * b_re - coef_im[..., None] * b_im
    bbar_im = coef_re[..., None] * b_im + coef_im[..., None] * b_re

    gps = S5_SLAB_IN // S5_GROUP
    eye = jnp.eye(gps, dtype=F32)

    def in_slabs(bb):
        w = jnp.einsum('kgpc,gh->kgchp', bb.reshape(S5_SLABS, gps, S5_STATE, S5_GROUP), eye)
        return w.reshape(S5_SLABS, S5_SLAB_IN, S5_SLAB_ST).astype(BF16)

    def out_slabs(cc):
        w = jnp.einsum('kgcp,gh->kgphc', cc.reshape(S5_SLABS, gps, S5_GROUP, S5_STATE), eye)
        return w.reshape(S5_SLABS, S5_SLAB_ST, S5_SLAB_IN).astype(BF16)

    dec = (lam_re * step).reshape(1, S5_STATES)
    ang = (lam_im * step).reshape(1, S5_STATES)

    def power(n):
        m = jnp.exp(n * dec)
        return m * jnp.cos(n * ang), m * jnp.sin(n * ang)

    idx = jnp.arange(S5_T, dtype=F32)[:, None]
    pn_re, pn_im = power(-idx)
    pp_re, pp_im = power(idx)
    pc_re, pc_im = power(idx + 1.0)
    return dict(wb_re=in_slabs(bbar_re), wb_im=in_slabs(bbar_im),
                pn_re=pn_re, pn_im=pn_im, pp_re=pp_re, pp_im=pp_im, pc_re=pc_re, pc_im=pc_im,
                c_re=out_slabs(c_re), c_im=out_slabs(c_im), d=d.reshape(1, -1),
                w_glu=w_glu.astype(BF16), b_glu=b_glu.reshape(1, -1))


def _causal_conv(buf, x, w_ref, rows):
    buf[pl.ds(SUBLANES, rows), :] = x
    acc = None
    for j in range(CONV_K):
        tap = buf[pl.ds(SUBLANES - (CONV_K - 1) + j, rows), :] * w_ref[j:j + 1, :]
        acc = tap if acc is None else acc + tap
    buf[pl.ds(0, SUBLANES), :] = buf[pl.ds(rows, SUBLANES), :]
    return acc


def _tri_masks(q):
    row = lax.broadcasted_iota(jnp.int32, (q, q), 0)
    col = lax.broadcasted_iota(jnp.int32, (q, q), 1)
    return row >= col, row > col, row == col


def _ssd_kernel(xz_ref, sm_ref, cw_ref, cb_ref, dtb_ref, nega_ref, d_ref, ng_ref, out_ref,
                buf, state, *, q):
    @pl.when(pl.program_id(1) == 0)
    def _():
        buf[pl.ds(0, SUBLANES), :] = jnp.zeros((SUBLANES, SSD_CONV_WIDTH), F32)
        state[...] = jnp.zeros_like(state)

    xbc = xz_ref[:, :SSD_CONV_WIDTH].astype(F32)
    xbc = _silu(_causal_conv(buf, xbc, cw_ref, q) + cb_ref[...])
    xs = xbc[:, :BRANCH_WIDTH]
    bmat = xbc[:, BRANCH_WIDTH:BRANCH_WIDTH + SSD_BC]
    cmat = xbc[:, BRANCH_WIDTH + SSD_BC:]

    causal, _, _ = _tri_masks(q)
    tril = jnp.where(causal, 1.0, 0.0)
    dt = _softplus(sm_ref[...] + dtb_ref[...])
    da = dt * nega_ref[...]
    acum = _dot_f32(tril, da)
    acum_t = acum.T
    alast = acum[q - 1:q, :]
    e_in = jnp.exp(acum)
    e_out = jnp.exp(alast - acum)
    e_all = jnp.exp(alast)

    ys = []
    for g in range(SSD_GROUPS):
        bg = bmat[:, g * SSD_STATE:(g + 1) * SSD_STATE].astype(BF16)
        cg = cmat[:, g * SSD_STATE:(g + 1) * SSD_STATE].astype(BF16)
        scores = _dot_nt(cg, bg)
        st = state[g]
        cs = _dot(cg, st.astype(BF16))
        xdw, dec = [], []
        for hh in range(SSD_HPG):
            h = g * SSD_HPG + hh
            col = slice(h * SSD_HEAD_DIM, (h + 1) * SSD_HEAD_DIM)
            xh = xs[:, col]
            xd = xh * dt[:, h:h + 1]
            lmat = jnp.where(causal, jnp.exp(acum[:, h:h + 1] - acum_t[h:h + 1, :]), 0.0)
            y = _dot((scores * lmat).astype(BF16), xd.astype(BF16))
            y = y + cs[:, hh * SSD_HEAD_DIM:(hh + 1) * SSD_HEAD_DIM] * e_in[:, h:h + 1]
            ys.append(y + xh * d_ref[:, col])
            xdw.append(xd * e_out[:, h:h + 1])
            dec.append(jnp.broadcast_to(e_all[:, h:h + 1], (1, SSD_HEAD_DIM)))
        xdw = jnp.concatenate(xdw, axis=-1).astype(BF16)
        dec = jnp.concatenate(dec, axis=-1)
        state[g] = st * dec + _dot(bmat[:, g * SSD_STATE:(g + 1) * SSD_STATE].T.astype(BF16), xdw)

    y = jnp.concatenate(ys, axis=-1) * _silu(xz_ref[:, SSD_CONV_WIDTH:].astype(F32))
    y = y * lax.rsqrt(jnp.mean(y * y, axis=-1, keepdims=True) + LN_EPS) * ng_ref[...]
    out_ref[...] = y.astype(out_ref.dtype)


def _ssd_branch(xz, small, prm, bsz, lp):
    q = CHUNK
    nblk = lp // q
    wx = SSD_CONV_WIDTH + BRANCH_WIDTH
    rowmap = lambda b, r: (b * nblk + r, 0)
    c2 = lambda b, r: (0, 0)
    return pl.pallas_call(
        functools.partial(_ssd_kernel, q=q),
        grid=(bsz, nblk),
        in_specs=[pl.BlockSpec((q, wx), rowmap), pl.BlockSpec((q, LANES), rowmap),
                  pl.BlockSpec((CONV_K, SSD_CONV_WIDTH), c2), pl.BlockSpec((1, SSD_CONV_WIDTH), c2),
                  pl.BlockSpec((1, LANES), c2), pl.BlockSpec((1, LANES), c2),
                  pl.BlockSpec((1, BRANCH_WIDTH), c2), pl.BlockSpec((1, BRANCH_WIDTH), c2)],
        out_specs=pl.BlockSpec((q, BRANCH_WIDTH), rowmap),
        out_shape=jax.ShapeDtypeStruct((bsz * lp, BRANCH_WIDTH), BF16),
        scratch_shapes=[pltpu.VMEM((q + SUBLANES, SSD_CONV_WIDTH), F32),
                        pltpu.VMEM((SSD_GROUPS, SSD_STATE, SSD_HPG * SSD_HEAD_DIM), F32)],
        compiler_params=_params("parallel", "arbitrary"),
        name="ssd_branch",
    )(xz, small, prm["conv_w"], prm["conv_b"], prm["dt_bias"], prm["neg_a"], prm["d"], prm["norm_g"])


def _lane_vector(values, offset):
    n = values.shape[0]
    return jnp.pad(values.astype(F32), (offset, LANES - offset - n)).reshape(1, LANES)


def _ssd_prepare(conv_w, conv_b, dt_bias, a_log, d, norm_g):
    return dict(conv_w=conv_w, conv_b=conv_b.reshape(1, -1),
                dt_bias=_lane_vector(dt_bias, SM_SSD_DT),
                neg_a=_lane_vector(-jnp.exp(a_log), SM_SSD_DT),
                d=jnp.repeat(d, SSD_HEAD_DIM).reshape(1, -1), norm_g=norm_g.reshape(1, -1))


def _gdn_kernel(qkvz_ref, sm_ref, cw_ref, dtb_ref, nega_ref, ng_ref, out_ref, buf, state, *, q):
    @pl.when(pl.program_id(1) == 0)
    def _():
        buf[pl.ds(0, SUBLANES), :] = jnp.zeros((SUBLANES, GDN_QKV), F32)
        state[...] = jnp.zeros_like(state)

    qkv = _silu(_causal_conv(buf, qkvz_ref[:, :GDN_QKV].astype(F32), cw_ref, q))

    causal, strict, diag = _tri_masks(q)
    tril = jnp.where(causal, 1.0, 0.0)
    eye = jnp.where(diag, 1.0, 0.0)
    sm = sm_ref[...]
    glog = nega_ref[...] * _softplus(sm + dtb_ref[...])
    gcum = _dot_f32(tril, glog)
    gcum_t = gcum.T
    beta_all = _sigmoid(sm)
    e_in = jnp.exp(gcum)
    glast = gcum[q - 1:q, :]
    e_out = jnp.exp(glast - gcum)
    e_all = jnp.exp(glast)

    heads = range(GDN_HEADS)
    dk = GDN_HEAD_DIM
    qs, ks, vs, betas, gammas, pws, tinvs, attns = [], [], [], [], [], [], [], []
    for h in heads:
        qh = qkv[:, h * dk:(h + 1) * dk]
        kh = qkv[:, BRANCH_WIDTH + h * dk:BRANCH_WIDTH + (h + 1) * dk]
        qs.append(qh * lax.rsqrt(jnp.sum(qh * qh, axis=-1, keepdims=True) + 1e-6) * (dk ** -0.5))
        ks.append(kh * lax.rsqrt(jnp.sum(kh * kh, axis=-1, keepdims=True) + 1e-6))
        vs.append(qkv[:, 2 * BRANCH_WIDTH + h * dk:2 * BRANCH_WIDTH + (h + 1) * dk])
        betas.append(beta_all[:, SM_GDN_B + h:SM_GDN_B + h + 1])
        la = SM_GDN_A + h
        gammas.append(jnp.where(causal, jnp.exp(gcum[:, la:la + 1] - gcum_t[la:la + 1, :]), 0.0))
    for h in heads:
        kb = ks[h].astype(BF16)
        kq = _dot_nt(jnp.concatenate([kb, qs[h].astype(BF16)], axis=0), kb)
        pws.append(jnp.where(strict, -(kq[:q] * gammas[h] * betas[h]), 0.0))
        attns.append((kq[q:] * gammas[h]).astype(BF16))
    for h in heads:
        tinvs.append(eye + pws[h])
        pwb = pws[h].astype(BF16)
        pws[h] = _dot(pwb, pwb)
    steps = int(math.log2(q)) - 1
    for i in range(steps):
        last = i == steps - 1
        for h in heads:
            pwb = pws[h].astype(BF16)
            lhs = tinvs[h].astype(BF16) if last else jnp.concatenate([tinvs[h].astype(BF16), pwb], axis=0)
            both = _dot(lhs, pwb)
            tinvs[h] = tinvs[h] + both[:q]
            if not last:
                pws[h] = both[q:]
    us, ws = [], []
    for h in heads:
        la = SM_GDN_A + h
        rhs = jnp.concatenate([vs[h] * betas[h], ks[h] * (betas[h] * e_in[:, la:la + 1])], axis=-1)
        uw = _dot(tinvs[h].astype(BF16), rhs.astype(BF16))
        us.append(uw[:, :dk])
        ws.append(uw[:, dk:])
    sts, wq = [], []
    for h in heads:
        la = SM_GDN_A + h
        sts.append(state[h])
        lhs = jnp.concatenate([ws[h].astype(BF16), (qs[h] * e_in[:, la:la + 1]).astype(BF16)], axis=0)
        wq.append(_dot(lhs, sts[h].astype(BF16)))
    outs = []
    for h in heads:
        la = SM_GDN_A + h
        vb = (us[h] - wq[h][:q]).astype(BF16)
        o = wq[h][q:] + _dot(attns[h], vb)
        kd = ks[h] * e_out[:, la:la + 1]
        state[h] = sts[h] * e_all[:, la:la + 1] + _dot(kd.T.astype(BF16), vb)
        outs.append(o * lax.rsqrt(jnp.mean(o * o, axis=-1, keepdims=True) + LN_EPS) * ng_ref[...])

    o = jnp.concatenate(outs, axis=-1) * _silu(qkvz_ref[:, GDN_QKV:].astype(F32))
    out_ref[...] = o.astype(out_ref.dtype)


def _gdn_branch(qkvz, small, prm, bsz, lp):
    q = CHUNK
    nblk = lp // q
    wx = GDN_QKV + BRANCH_WIDTH
    rowmap = lambda b, r: (b * nblk + r, 0)
    c2 = lambda b, r: (0, 0)
    return pl.pallas_call(
        functools.partial(_gdn_kernel, q=q),
        grid=(bsz, nblk),
        in_specs=[pl.BlockSpec((q, wx), rowmap), pl.BlockSpec((q, LANES), rowmap),
                  pl.BlockSpec((CONV_K, GDN_QKV), c2),
                  pl.BlockSpec((1, LANES), c2), pl.BlockSpec((1, LANES), c2),
                  pl.BlockSpec((1, GDN_HEAD_DIM), c2)],
        out_specs=pl.BlockSpec((q, BRANCH_WIDTH), rowmap),
        out_shape=jax.ShapeDtypeStruct((bsz * lp, BRANCH_WIDTH), BF16),
        scratch_shapes=[pltpu.VMEM((q + SUBLANES, GDN_QKV), F32),
                        pltpu.VMEM((GDN_HEADS, GDN_HEAD_DIM, GDN_HEAD_DIM), F32)],
        compiler_params=_params("parallel", "arbitrary"),
        name="gdn_branch",
    )(qkvz, small, prm["conv_w"], prm["dt_bias"], prm["neg_a"], prm["norm_g"])


def _gdn_prepare(conv_w, dt_bias, a_log, norm_g):
    return dict(conv_w=conv_w, dt_bias=_lane_vector(dt_bias, SM_GDN_A),
                neg_a=_lane_vector(-jnp.exp(a_log), SM_GDN_A), norm_g=norm_g.reshape(1, -1))


def _merge_kernel(h_ref, ya_ref, yb_ref, yc_ref, wg_ref, bg_ref, wbr_ref, wo_ref, g_ref, b_ref,
                  o_ref, ob_ref, *, alpha):
    h = h_ref[...]
    logits = _dot(h.astype(BF16), wg_ref[...]) + bg_ref[...]
    merged = None
    for n, y_ref in enumerate((ya_ref, yb_ref, yc_ref)):
        gate = _sigmoid(logits[:, n * D_MODEL:(n + 1) * D_MODEL])
        term = gate * _dot(y_ref[...], wbr_ref[n])
        merged = term if merged is None else merged + term
    y = alpha * h + _dot(merged.astype(BF16), wo_ref[...])
    y = _ln_math(y, g_ref[...], b_ref[...])
    o_ref[...] = y
    ob_ref[...] = y.astype(BF16)


def _merge(h, ya, yb, yc, prm, alpha):
    m, d = h.shape
    tm = _pick(m, (512, 256, 128))
    rows = lambda w: pl.BlockSpec((tm, w), lambda i: (i, 0))
    return pl.pallas_call(
        functools.partial(_merge_kernel, alpha=alpha),
        grid=(m // tm,),
        in_specs=[rows(d), rows(BRANCH_WIDTH), rows(BRANCH_WIDTH), rows(BRANCH_WIDTH),
                  _full((d, N_BRANCH * d)), _full((1, N_BRANCH * d)),
                  _full((N_BRANCH, BRANCH_WIDTH, d)), _full((d, d)), _full((1, d)), _full((1, d))],
        out_specs=[rows(d), rows(d)],
        out_shape=[jax.ShapeDtypeStruct((m, d), F32), jax.ShapeDtypeStruct((m, d), BF16)],
        compiler_params=_params("parallel"),
        name="merge",
    )(h, ya, yb, yc, prm["w_gate"], prm["b_gate"], prm["w_branch"], prm["w_out"], prm["ln_g"], prm["ln_b"])


def _split_in_proj(w):
    widths = (BRANCH_WIDTH, BRANCH_WIDTH, SSD_CONV_WIDTH, SSD_HEADS, BRANCH_WIDTH,
              GDN_QKV, GDN_HEADS, GDN_HEADS, BRANCH_WIDTH, N_BRANCH * D_MODEL)
    offs = [0]
    for wd in widths:
        offs.append(offs[-1] + wd)
    seg = [w[:, offs[i]:offs[i + 1]] for i in range(len(widths))]
    s5_u, s5_z, ssd_xbc, ssd_dt, ssd_z, gdn_qkv, gdn_a, gdn_b, gdn_z, gate = seg
    pad = jnp.zeros((w.shape[0], LANES - SSD_HEADS - 2 * GDN_HEADS), w.dtype)
    return dict(
        s5=jnp.concatenate([s5_u, s5_z], axis=1).astype(BF16),
        ssd=jnp.concatenate([ssd_xbc, ssd_z], axis=1).astype(BF16),
        gdn=jnp.concatenate([gdn_qkv, gdn_z], axis=1).astype(BF16),
        small=jnp.concatenate([ssd_dt, gdn_a, gdn_b, pad], axis=1).astype(BF16),
        gate=gate.astype(BF16))


def kernel(x, meta, ln_in_g, ln_in_b, w_in, s5_a_re, s5_a_im, s5_log_step, s5_b_re, s5_b_im, s5_c_re, s5_c_im, s5_d, s5_w_glu, s5_b_glu, ssd_conv_w, ssd_conv_b, ssd_dt_bias, ssd_a_log, ssd_d, ssd_norm_g, gdn_conv_w, gdn_dt_bias, gdn_a_log, gdn_norm_g, w_branch, b_gate, w_out, ln_g, ln_b):
    bsz, seq, d = x.shape
    depth = w_in.shape[0]
    alpha = (2 * depth) ** 0.25
    lr = N_META + seq
    lp = -(-lr // CHUNK) * CHUNK
    h0 = jnp.concatenate([jnp.broadcast_to(meta[None].astype(x.dtype), (bsz, N_META, d)), x,
                          jnp.zeros((bsz, lp - lr, d), x.dtype)], axis=1).reshape(bsz * lp, d)
    h, hb = _layer_norm_in(h0, ln_in_g, ln_in_b)

    for layer in range(depth):
        wi = _split_in_proj(w_in[layer])
        s5p = _s5_prepare(s5_a_re[layer], s5_a_im[layer], s5_log_step[layer], s5_b_re[layer], s5_b_im[layer],
                          s5_c_re[layer], s5_c_im[layer], s5_d[layer], s5_w_glu[layer], s5_b_glu[layer])
        ssdp = _ssd_prepare(ssd_conv_w[layer], ssd_conv_b[layer], ssd_dt_bias[layer], ssd_a_log[layer],
                            ssd_d[layer], ssd_norm_g[layer])
        gdnp = _gdn_prepare(gdn_conv_w[layer], gdn_dt_bias[layer], gdn_a_log[layer], gdn_norm_g[layer])
        mrg = dict(w_gate=wi["gate"], b_gate=b_gate[layer].reshape(1, -1),
                   w_branch=w_branch[layer].astype(BF16), w_out=w_out[layer].astype(BF16),
                   ln_g=ln_g[layer].reshape(1, -1), ln_b=ln_b[layer].reshape(1, -1))

        small = _matmul(hb, wi["small"], F32, "proj_small")
        y_a = _s5_branch(_matmul(hb, wi["s5"], BF16, "proj_s5"), s5p, bsz, lp)
        y_b = _ssd_branch(_matmul(hb, wi["ssd"], BF16, "proj_ssd"), small, ssdp, bsz, lp)
        y_c = _gdn_branch(_matmul(hb, wi["gdn"], BF16, "proj_gdn"), small, gdnp, bsz, lp)
        h, hb = _merge(h, y_a, y_b, y_c, mrg, alpha)

    return h.reshape(bsz, lp, d)[:, N_META:lr]
```

```python
import functools
import math

import numpy as np
import jax
import jax.numpy as jnp
from jax import lax
from jax.experimental import pallas as pl
from jax.experimental.pallas import tpu as pltpu

D_MODEL = 1024
N_META = 16
CONV_K = 4
N_BRANCH = 3
BRANCH_WIDTH = 768

S5_GROUP = 16
S5_GROUPS = BRANCH_WIDTH // S5_GROUP
S5_STATE = 64
S5_STATES = S5_GROUPS * S5_STATE
S5_SLAB_IN = 256
S5_SLAB_ST = 1024
S5_SLABS = BRANCH_WIDTH // S5_SLAB_IN
S5_T = 64
S5_CARRY_ROWS = 16

SSD_HEAD_DIM = 64
SSD_HEADS = 12
SSD_GROUPS = 2
SSD_HPG = SSD_HEADS // SSD_GROUPS
SSD_STATE = 128
SSD_BC = SSD_GROUPS * SSD_STATE
SSD_CONV_WIDTH = BRANCH_WIDTH + 2 * SSD_BC
SSD_GW = SSD_HPG * SSD_HEAD_DIM

GDN_HEAD_DIM = 128
GDN_HEADS = 6
GDN_QKV = 3 * BRANCH_WIDTH

CHUNK = 128
LANES = 128
CONV_CARRY = 16
LN_EPS = 1e-5

SM_SSD_DT = 0
SM_GDN_A = SSD_HEADS
SM_GDN_B = SSD_HEADS + GDN_HEADS

VMEM_LIMIT = 56 * 1024 * 1024

F32 = jnp.float32
BF16 = jnp.bfloat16


def _dot(a, b):
    return jnp.dot(a, b, preferred_element_type=F32)


def _cumsum_rows(tril, x):
    parts = _dot(tril, _split3(x))
    return parts[:, :LANES] + parts[:, LANES:2 * LANES] + parts[:, 2 * LANES:]


def _dot_nt(a, b):
    return lax.dot_general(a, b, (((1,), (1,)), ((), ())), preferred_element_type=F32)


def _sigmoid(x):
    return 1.0 / (1.0 + jnp.exp(-x))


def _silu(x):
    return x * _sigmoid(x)


def _softplus(x):
    return jnp.maximum(x, 0.0) + jnp.log1p(jnp.exp(-jnp.abs(x)))


def _gelu_tanh(x):
    c = math.sqrt(2.0 / math.pi)
    return 0.5 * x * (1.0 + jnp.tanh(c * (x + 0.044715 * (x * x * x))))


def _bf16_floor(x):
    bits = lax.bitcast_convert_type(x, jnp.uint32) & jnp.uint32(0xFFFF0000)
    return lax.bitcast_convert_type(bits, F32)


def _split3(x):
    hi = _bf16_floor(x)
    r1 = x - hi
    mid = _bf16_floor(r1)
    return jnp.concatenate([hi.astype(BF16), mid.astype(BF16), (r1 - mid).astype(BF16)], axis=-1)


def _split2(x):
    hi = _bf16_floor(x)
    return jnp.concatenate([hi.astype(BF16), (x - hi).astype(BF16)], axis=-1)


def _pick(n, candidates):
    for c in candidates:
        if n % c == 0:
            return c
    raise ValueError(f"no tile for {n} in {candidates}")


def _params(*semantics):
    return pltpu.CompilerParams(dimension_semantics=semantics, vmem_limit_bytes=VMEM_LIMIT)


def _full(shape):
    n = len(shape)
    return pl.BlockSpec(shape, lambda *_: (0,) * n)


def _layer_block(shape, layer):
    n = len(shape)
    return pl.BlockSpec((None,) + tuple(shape), lambda *_: (layer,) + (0,) * n)


def _expand_matrix(first_lane, heads, width, pieces):
    e = np.zeros((LANES, heads * width), np.float32)
    for h in range(heads):
        e[first_lane + h, h * width:(h + 1) * width] = 1.0
    return jnp.asarray(np.tile(e, (pieces, 1)), BF16)


def _shift_matrix(q):
    s = np.zeros(((CONV_K - 1) * q, q + CONV_CARRY), np.float32)
    for j in range(CONV_K - 1):
        for t in range(q):
            s[j * q + t, CONV_CARRY + t - (CONV_K - 1) + j] = 1.0
    return jnp.asarray(s, BF16)


def _block_ones(width, block, pieces):
    o = np.kron(np.eye(width // block, dtype=np.float32), np.ones((block, block), np.float32))
    return jnp.asarray(np.tile(o, (pieces, 1)), BF16)


def _ln_math(x, g, b):
    mu = jnp.mean(x, axis=-1, keepdims=True)
    xc = x - mu
    var = jnp.mean(xc * xc, axis=-1, keepdims=True)
    return xc * lax.rsqrt(var + LN_EPS) * g + b


def _ln_kernel(x_ref, g_ref, b_ref, o_ref, ob_ref):
    y = _ln_math(x_ref[...], g_ref[...], b_ref[...])
    o_ref[...] = y
    ob_ref[...] = y.astype(BF16)


def _layer_norm_in(x2d, g, b):
    m, d = x2d.shape
    tm = _pick(m, (512, 256, 128))
    return pl.pallas_call(
        _ln_kernel,
        grid=(m // tm,),
        in_specs=[pl.BlockSpec((tm, d), lambda i: (i, 0)), _full((1, d)), _full((1, d))],
        out_specs=[pl.BlockSpec((tm, d), lambda i: (i, 0)), pl.BlockSpec((tm, d), lambda i: (i, 0))],
        out_shape=[jax.ShapeDtypeStruct((m, d), F32), jax.ShapeDtypeStruct((m, d), BF16)],
        compiler_params=_params("parallel"),
        name="ln_in",
    )(x2d, g.reshape(1, d), b.reshape(1, d))


def _mm_kernel(x_ref, w_ref, o_ref):
    o_ref[...] = _dot(x_ref[...], w_ref[...]).astype(o_ref.dtype)


def _matmul(x, w, layer, out_dtype, name):
    m, k = x.shape
    n = w.shape[-1]
    tm = _pick(m, (1536, 1024, 512, 256, 128))
    return pl.pallas_call(
        _mm_kernel,
        grid=(m // tm,),
        in_specs=[pl.BlockSpec((tm, k), lambda i: (i, 0)), _layer_block((k, n), layer)],
        out_specs=pl.BlockSpec((tm, n), lambda i: (i, 0)),
        out_shape=jax.ShapeDtypeStruct((m, n), out_dtype),
        compiler_params=_params("parallel"),
        name=name,
    )(x, w)


def _carry_rows(c):
    hi = _bf16_floor(c)
    r1 = c - hi
    mid = _bf16_floor(r1)
    lo = r1 - mid
    rid = lax.broadcasted_iota(jnp.int32, (S5_CARRY_ROWS, c.shape[1]), 0)
    ext = jnp.where(rid == 0, hi, jnp.where(rid == 1, mid, jnp.where(rid == 2, lo, 0.0)))
    return ext.astype(BF16)


def _s5_kernel(uz_ref, wbre_ref, wbim_ref, pnr_ref, pni_ref, ppr_ref, ppi_ref, abr_ref, abi_ref,
               plr_ref, pli_ref, cre_ref, cim_ref, d_ref, wglu_ref, bglu_ref, out_ref,
               bure, buim, sre, sim, st_re, st_im, *, rb):
    @pl.when(pl.program_id(1) == 0)
    def _():
        st_re[...] = jnp.zeros_like(st_re)
        st_im[...] = jnp.zeros_like(st_im)

    u = uz_ref[:, :BRANCH_WIDTH]
    for k in range(S5_SLABS):
        uk = u[:, k * S5_SLAB_IN:(k + 1) * S5_SLAB_IN]
        sl = slice(k * S5_SLAB_ST, (k + 1) * S5_SLAB_ST)
        bure[:, sl] = _dot(uk, wbre_ref[k]).astype(BF16)
        buim[:, sl] = _dot(uk, wbim_ref[k]).astype(BF16)

    t = S5_T
    row = lax.broadcasted_iota(jnp.int32, (t, t + S5_CARRY_ROWS), 0)
    col = lax.broadcasted_iota(jnp.int32, (t, t + S5_CARRY_ROWS), 1)
    ones_cols = jnp.where(col >= t, 1.0, 0.0) * jnp.where(col < t + 3, 1.0, 0.0)
    tril = (jnp.where(row >= col, 1.0, 0.0) + ones_cols).astype(BF16)

    for c in range(rb // t):
        rows = slice(c * t, (c + 1) * t)
        for k in range(S5_SLABS):
            sl = slice(k * S5_SLAB_ST, (k + 1) * S5_SLAB_ST)
            br = bure[rows, sl]
            bi = buim[rows, sl]
            pnr = pnr_ref[:, sl]
            pni = pni_ref[:, sl]
            s_in_r = st_re[:, sl]
            s_in_i = st_im[:, sl]
            abr = abr_ref[:, sl]
            abi = abi_ref[:, sl]
            xr = jnp.concatenate([br * pnr - bi * pni, _carry_rows(abr * s_in_r - abi * s_in_i)], axis=0)
            xi = jnp.concatenate([br * pni + bi * pnr, _carry_rows(abr * s_in_i + abi * s_in_r)], axis=0)
            acc_r = _dot(tril, xr)
            acc_i = _dot(tril, xi)
            lr = acc_r[t - 1:t, :]
            li = acc_i[t - 1:t, :]
            plr = plr_ref[:, sl]
            pli = pli_ref[:, sl]
            st_re[:, sl] = lr * plr - li * pli
            st_im[:, sl] = lr * pli + li * plr
            ar = acc_r.astype(BF16)
            ai = acc_i.astype(BF16)
            ppr = ppr_ref[:, sl]
            ppi = ppi_ref[:, sl]
            sre[rows, sl] = ar * ppr - ai * ppi
            sim[rows, sl] = ar * ppi + ai * ppr

    ys = []
    for k in range(S5_SLABS):
        sl = slice(k * S5_SLAB_ST, (k + 1) * S5_SLAB_ST)
        ys.append(_dot(sre[:, sl], cre_ref[k]) - _dot(sim[:, sl], cim_ref[k]))
    y = jnp.concatenate(ys, axis=-1) + d_ref[...] * u.astype(F32)
    v = _gelu_tanh(y)
    v = v * _sigmoid(_dot(v.astype(BF16), wglu_ref[...]) + bglu_ref[...])
    z = uz_ref[:, BRANCH_WIDTH:].astype(F32)
    out_ref[...] = (v * _silu(z)).astype(out_ref.dtype)


def _s5_branch(uz, prm, layer, bsz, lp):
    rb = _pick(lp, (384, 256, 128))
    nblk = lp // rb
    w2 = 2 * BRANCH_WIDTH
    lb = functools.partial(_layer_block, layer=layer)
    tab = lb((S5_T, S5_STATES))
    rowv = lb((1, S5_STATES))
    wb = lb((S5_SLABS, S5_SLAB_IN, S5_SLAB_ST))
    cb = lb((S5_SLABS, S5_SLAB_ST, S5_SLAB_IN))
    vec = lb((1, BRANCH_WIDTH))
    return pl.pallas_call(
        functools.partial(_s5_kernel, rb=rb),
        grid=(bsz, nblk),
        in_specs=[pl.BlockSpec((rb, w2), lambda b, r: (b * nblk + r, 0)),
                  wb, wb, tab, tab, tab, tab, rowv, rowv, rowv, rowv, cb, cb, vec,
                  lb((BRANCH_WIDTH, BRANCH_WIDTH)), vec],
        out_specs=pl.BlockSpec((rb, BRANCH_WIDTH), lambda b, r: (b * nblk + r, 0)),
        out_shape=jax.ShapeDtypeStruct((bsz * lp, BRANCH_WIDTH), BF16),
        scratch_shapes=[pltpu.VMEM((rb, S5_STATES), BF16), pltpu.VMEM((rb, S5_STATES), BF16),
                        pltpu.VMEM((rb, S5_STATES), BF16), pltpu.VMEM((rb, S5_STATES), BF16),
                        pltpu.VMEM((1, S5_STATES), F32), pltpu.VMEM((1, S5_STATES), F32)],
        compiler_params=_params("parallel", "arbitrary"),
        name="s5_branch",
    )(uz, prm["wb_re"], prm["wb_im"], prm["pn_re"], prm["pn_im"], prm["pp_re"], prm["pp_im"],
      prm["ab_re"], prm["ab_im"], prm["pl_re"], prm["pl_im"], prm["c_re"], prm["c_im"],
      prm["d"], prm["w_glu"], prm["b_glu"])


def _s5_prepare(a_re, a_im, log_step, b_re, b_im, c_re, c_im, d, w_glu, b_glu):
    nl = a_re.shape[0]
    lam_re = jnp.minimum(a_re, -1e-4)
    lam_im = a_im
    step = jnp.exp(log_step)[..., None]
    dec = lam_re * step
    ang = lam_im * step
    mag = jnp.exp(dec)
    abar_re, abar_im = mag * jnp.cos(ang), mag * jnp.sin(ang)
    den = lam_re * lam_re + lam_im * lam_im
    nr, ni = abar_re - 1.0, abar_im
    coef_re = (nr * lam_re + ni * lam_im) / den
    coef_im = (ni * lam_re - nr * lam_im) / den
    bbar_re = coef_re[..., None] * b_re - coef_im[..., None] * b_im
    bbar_im = coef_re[..., None] * b_im + coef_im[..., None] * b_re

    gps = S5_SLAB_IN // S5_GROUP
    eye = jnp.eye(gps, dtype=F32)

    def in_slabs(bb):
        w = jnp.einsum('lkgpc,gh->lkgchp', bb.reshape(nl, S5_SLABS, gps, S5_STATE, S5_GROUP), eye)
        return w.reshape(nl, S5_SLABS, S5_SLAB_IN, S5_SLAB_ST).astype(BF16)

    def out_slabs(cc):
        w = jnp.einsum('lkgcp,gh->lkgphc', cc.reshape(nl, S5_SLABS, gps, S5_GROUP, S5_STATE), eye)
        return w.reshape(nl, S5_SLABS, S5_SLAB_ST, S5_SLAB_IN).astype(BF16)

    idx = jnp.arange(S5_T, dtype=F32)[None, :, None]
    dec = dec.reshape(nl, 1, S5_STATES)
    ang = ang.reshape(nl, 1, S5_STATES)
    cs, sn = jnp.cos(idx * ang), jnp.sin(idx * ang)
    grow, shrink = jnp.exp(idx * dec), jnp.exp(-idx * dec)
    pp_re, pp_im = grow * cs, grow * sn
    return dict(wb_re=in_slabs(bbar_re), wb_im=in_slabs(bbar_im),
                pn_re=(shrink * cs).astype(BF16), pn_im=(-shrink * sn).astype(BF16),
                pp_re=pp_re.astype(BF16), pp_im=pp_im.astype(BF16),
                ab_re=abar_re.reshape(nl, 1, S5_STATES), ab_im=abar_im.reshape(nl, 1, S5_STATES),
                pl_re=pp_re[:, S5_T - 1:], pl_im=pp_im[:, S5_T - 1:],
                c_re=out_slabs(c_re), c_im=out_slabs(c_im), d=d.reshape(nl, 1, -1),
                w_glu=w_glu.astype(BF16), b_glu=b_glu.reshape(nl, 1, -1))


def _conv_begin(buf, carry, x_ref, width, first):
    rb = x_ref.shape[0]

    @pl.when(first)
    def _():
        carry[...] = jnp.zeros_like(carry)

    buf[pl.ds(0, CONV_CARRY), :] = carry[...]
    buf[pl.ds(CONV_CARRY, rb), :] = x_ref[:, :width]
    carry[...] = x_ref[rb - CONV_CARRY:rb, :width]


def _causal_conv(buf, c, x, shift_ref, w_ref, q):
    taps = _dot(shift_ref[...], buf[pl.ds(c * q, q + CONV_CARRY), :])
    acc = x.astype(F32) * w_ref[CONV_K - 1:CONV_K, :]
    for j in range(CONV_K - 1):
        acc = acc + taps[j * q:(j + 1) * q, :] * w_ref[j:j + 1, :]
    return acc


def _tri_masks(q):
    row = lax.broadcasted_iota(jnp.int32, (q, q), 0)
    col = lax.broadcasted_iota(jnp.int32, (q, q), 1)
    return row >= col, row > col, row == col


def _ssd_kernel(xz_ref, sm_ref, shift_ref, cw_ref, cb_ref, dtb_ref, nega_ref, e64_ref, e128_ref,
                d_ref, ng_ref, out_ref, buf, carry, state, *, q, nc):
    first = pl.program_id(1) == 0

    @pl.when(first)
    def _():
        state[...] = jnp.zeros_like(state)

    _conv_begin(buf, carry, xz_ref, SSD_CONV_WIDTH, first)

    causal, _, _ = _tri_masks(q)
    tril = jnp.where(causal, 1.0, 0.0).astype(BF16)
    low_half = lax.broadcasted_iota(jnp.int32, (q, LANES), 1) < SSD_HEAD_DIM

    pre = []
    for c in range(nc):
        rows = slice(c * q, (c + 1) * q)
        xbc = _silu(_causal_conv(buf, c, xz_ref[rows, :SSD_CONV_WIDTH], shift_ref, cw_ref, q) + cb_ref[...])
        xs = xbc[:, :BRANCH_WIDTH]
        bmat = xbc[:, BRANCH_WIDTH:BRANCH_WIDTH + SSD_BC]
        cmat = xbc[:, BRANCH_WIDTH + SSD_BC:]
        dt = _softplus(sm_ref[rows, :] + dtb_ref[...])
        da = dt * nega_ref[...]
        acum = _cumsum_rows(tril, da)
        acum_t = acum.T
        a3 = _split3(acum)
        acum64 = _dot(a3, e64_ref[...])
        acum128 = _dot(a3, e128_ref[...])
        dt64 = _dot(_split3(dt), e64_ref[...])
        alast = acum64[q - 1:q, :]
        xd = xs * dt64
        bgs = [bmat[:, g * SSD_STATE:(g + 1) * SSD_STATE] for g in range(SSD_GROUPS)]
        cgs = [cmat[:, g * SSD_STATE:(g + 1) * SSD_STATE].astype(BF16) for g in range(SSD_GROUPS)]
        scores = [_dot_nt(cgs[g], bgs[g].astype(BF16)) for g in range(SSD_GROUPS)]
        ms, rhs = [], []
        for h in range(SSD_HEADS):
            lmat = jnp.where(causal, jnp.exp(acum128[:, h * LANES:(h + 1) * LANES] - acum_t[h:h + 1, :]), 0.0)
            ms.append((scores[h // SSD_HPG] * lmat).astype(BF16))
            pair = xd[:, (h // 2) * LANES:(h // 2 + 1) * LANES]
            keep = low_half if h % 2 == 0 else jnp.logical_not(low_half)
            rhs.append(jnp.where(keep, pair, 0.0).astype(BF16))
        ypairs = [_dot(ms[2 * p], rhs[2 * p]) + _dot(ms[2 * p + 1], rhs[2 * p + 1])
                  for p in range(SSD_HEADS // 2)]
        pre.append(dict(
            y=jnp.concatenate(ypairs, axis=-1) + xs * d_ref[...], cgs=cgs,
            bts=[bgs[g].T.astype(BF16) for g in range(SSD_GROUPS)],
            e_in=jnp.exp(acum64),
            xdw=(xd * jnp.exp(alast - acum64)).astype(BF16),
            e_all=jnp.exp(alast)))

    sts = [state[g] for g in range(SSD_GROUPS)]
    for c in range(nc):
        rows = slice(c * q, (c + 1) * q)
        pc = pre[c]
        y_off = jnp.concatenate([_dot(pc["cgs"][g], sts[g].astype(BF16)) for g in range(SSD_GROUPS)], axis=-1)
        for g in range(SSD_GROUPS):
            gs = slice(g * SSD_GW, (g + 1) * SSD_GW)
            sts[g] = sts[g] * pc["e_all"][:, gs] + _dot(pc["bts"][g], pc["xdw"][:, gs])
        y = (pc["y"] + y_off * pc["e_in"]) * _silu(xz_ref[rows, SSD_CONV_WIDTH:].astype(F32))
        y = y * lax.rsqrt(jnp.mean(y * y, axis=-1, keepdims=True) + LN_EPS) * ng_ref[...]
        out_ref[rows, :] = y.astype(out_ref.dtype)
    for g in range(SSD_GROUPS):
        state[g] = sts[g]


def _ssd_branch(xz, small, prm, layer, bsz, lp):
    q = CHUNK
    rb = _pick(lp, (3 * q, 2 * q, q))
    nblk = lp // rb
    wx = SSD_CONV_WIDTH + BRANCH_WIDTH
    rowmap = lambda b, r: (b * nblk + r, 0)
    lb = functools.partial(_layer_block, layer=layer)
    shift = _shift_matrix(q)
    e64 = _expand_matrix(SM_SSD_DT, SSD_HEADS, SSD_HEAD_DIM, 3)
    e128 = _expand_matrix(SM_SSD_DT, SSD_HEADS, LANES, 3)
    return pl.pallas_call(
        functools.partial(_ssd_kernel, q=q, nc=rb // q),
        grid=(bsz, nblk),
        in_specs=[pl.BlockSpec((rb, wx), rowmap), pl.BlockSpec((rb, LANES), rowmap),
                  _full(shift.shape), lb((CONV_K, SSD_CONV_WIDTH)), lb((1, SSD_CONV_WIDTH)),
                  lb((1, LANES)), lb((1, LANES)), _full(e64.shape), _full(e128.shape),
                  lb((1, BRANCH_WIDTH)), lb((1, BRANCH_WIDTH))],
        out_specs=pl.BlockSpec((rb, BRANCH_WIDTH), rowmap),
        out_shape=jax.ShapeDtypeStruct((bsz * lp, BRANCH_WIDTH), BF16),
        scratch_shapes=[pltpu.VMEM((rb + CONV_CARRY, SSD_CONV_WIDTH), BF16),
                        pltpu.VMEM((CONV_CARRY, SSD_CONV_WIDTH), BF16),
                        pltpu.VMEM((SSD_GROUPS, SSD_STATE, SSD_GW), F32)],
        compiler_params=_params("parallel", "arbitrary"),
        name="ssd_branch",
    )(xz, small, shift, prm["conv_w"], prm["conv_b"], prm["dt_bias"], prm["neg_a"], e64, e128,
      prm["d"], prm["norm_g"])


def _lane_vectors(values, offset):
    nl, n = values.shape
    return jnp.pad(values.astype(F32), ((0, 0), (offset, LANES - offset - n))).reshape(nl, 1, LANES)


def _ssd_prepare(conv_w, conv_b, dt_bias, a_log, d, norm_g):
    nl = conv_w.shape[0]
    return dict(conv_w=conv_w, conv_b=conv_b.reshape(nl, 1, -1),
                dt_bias=_lane_vectors(dt_bias, SM_SSD_DT),
                neg_a=_lane_vectors(-jnp.exp(a_log), SM_SSD_DT),
                d=jnp.repeat(d, SSD_HEAD_DIM, axis=1).reshape(nl, 1, -1), norm_g=norm_g.reshape(nl, 1, -1))


def _head_sums(x, ones_ref):
    outs = []
    for s in range(x.shape[1] // (2 * LANES)):
        xs = x[:, s * 2 * LANES:(s + 1) * 2 * LANES]
        outs.append(_dot(xs.astype(BF16), ones_ref[...]))
    return jnp.concatenate(outs, axis=-1)


def _gdn_kernel(qkvz_ref, sm_ref, shift_ref, cw_ref, dtb_ref, nega_ref, eg_ref, eb_ref, ones_ref, ng_ref,
                out_ref, buf, carry, state, *, q, nc, lr):
    first = pl.program_id(1) == 0

    @pl.when(first)
    def _():
        state[...] = jnp.zeros_like(state)

    _conv_begin(buf, carry, qkvz_ref, GDN_QKV, first)

    causal, strict, diag = _tri_masks(q)
    tril = jnp.where(causal, 1.0, 0.0).astype(BF16)
    eye = jnp.where(diag, 1.0, 0.0)

    pre = []
    for c in range(nc):
        rows = slice(c * q, (c + 1) * q)
        qkv = _silu(_causal_conv(buf, c, qkvz_ref[rows, :GDN_QKV], shift_ref, cw_ref, q))
        qa = qkv[:, :BRANCH_WIDTH]
        ka = qkv[:, BRANCH_WIDTH:2 * BRANCH_WIDTH]
        va = qkv[:, 2 * BRANCH_WIDTH:]
        qa = qa * (lax.rsqrt(_head_sums(qa * qa, ones_ref) + 1e-6) * (GDN_HEAD_DIM ** -0.5))
        ka = ka * lax.rsqrt(_head_sums(ka * ka, ones_ref) + 1e-6)
        sm = sm_ref[rows, :]
        glog = nega_ref[...] * _softplus(sm + dtb_ref[...])
        gcum = _cumsum_rows(tril, glog)
        g128 = _dot(_split3(gcum), eg_ref[...])
        beta = _dot(_split2(_sigmoid(sm)), eb_ref[...])
        rid = lax.broadcasted_iota(jnp.int32, beta.shape, 0) + (pl.program_id(1) * (nc * q) + c * q)
        beta = jnp.where(rid < lr, beta, 0.0)
        glast = g128[q - 1:q, :]
        e_in = jnp.exp(g128)
        pre.append(dict(qa=qa, ka=ka, beta=beta, g128=g128, gcum_t=gcum.T, e_all=jnp.exp(glast),
                        vbeta=va * beta, kbe=ka * (beta * e_in), qg=qa * e_in, kd=ka * jnp.exp(glast - g128)))

    dk = GDN_HEAD_DIM
    npair = GDN_HEADS // 2
    units = [(c, p) for c in range(nc) for p in range(npair)]
    ps = [slice(p * 2 * dk, (p + 1) * 2 * dk) for p in range(npair)]
    first = lax.broadcasted_iota(jnp.int32, (q, 2 * dk), 1) < dk

    def blockdiag(x):
        return jnp.concatenate([jnp.where(first, x, 0.0), jnp.where(first, 0.0, x)], axis=0).astype(BF16)

    causal2 = jnp.concatenate([causal, causal], axis=1)
    strict2 = jnp.concatenate([strict, strict], axis=1)
    eye2 = jnp.concatenate([eye, eye], axis=1)
    gammas, pws, tinvs, attns = {}, {}, {}, {}
    for c, p in units:
        la = SM_GDN_A + 2 * p
        gt = pre[c]["gcum_t"]
        grow = jnp.concatenate([gt[la:la + 1, :], gt[la + 1:la + 2, :]], axis=1)
        gammas[c, p] = jnp.where(causal2, jnp.exp(pre[c]["g128"][:, ps[p]] - grow), 0.0)
    for c, p in units:
        kp = pre[c]["ka"][:, ps[p]]
        lhs = jnp.concatenate([kp.astype(BF16), pre[c]["qa"][:, ps[p]].astype(BF16)], axis=0)
        kq = _dot_nt(lhs, blockdiag(kp))
        pws[c, p] = jnp.where(strict2, -(kq[:q] * gammas[c, p] * pre[c]["beta"][:, ps[p]]), 0.0)
        attns[c, p] = (kq[q:] * gammas[c, p]).astype(BF16)
    for u in units:
        tinvs[u] = eye2 + pws[u]
        pws[u] = _dot(pws[u].astype(BF16), blockdiag(pws[u]))
    steps = int(math.log2(q)) - 1
    for i in range(steps):
        last = i == steps - 1
        for u in units:
            pwb = pws[u].astype(BF16)
            lhs = tinvs[u].astype(BF16) if last else jnp.concatenate([tinvs[u].astype(BF16), pwb], axis=0)
            both = _dot(lhs, blockdiag(pws[u]))
            tinvs[u] = tinvs[u] + both[:q]
            if not last:
                pws[u] = both[q:]
    us, ws = {}, {}
    for c, p in units:
        uw = []
        for i in range(2):
            hcol = slice((2 * p + i) * dk, (2 * p + i + 1) * dk)
            rhs = jnp.concatenate([pre[c]["vbeta"][:, hcol], pre[c]["kbe"][:, hcol]], axis=-1)
            uw.append(_dot(tinvs[c, p][:, i * dk:(i + 1) * dk].astype(BF16), rhs.astype(BF16)))
        us[c, p] = jnp.concatenate([uw[0][:, :dk], uw[1][:, :dk]], axis=-1)
        ws[c, p] = jnp.concatenate([uw[0][:, dk:], uw[1][:, dk:]], axis=-1)

    sts = [state[h] for h in range(GDN_HEADS)]
    zero = jnp.zeros((dk, dk), F32)
    for c in range(nc):
        wq, outs = [], []
        for p in range(npair):
            sbd = jnp.concatenate([jnp.concatenate([sts[2 * p], zero], axis=1),
                                   jnp.concatenate([zero, sts[2 * p + 1]], axis=1)], axis=0)
            lhs = jnp.concatenate([ws[c, p].astype(BF16), pre[c]["qg"][:, ps[p]].astype(BF16)], axis=0)
            wq.append(_dot(lhs, sbd.astype(BF16)))
        for p in range(npair):
            vnew = us[c, p] - wq[p][:q]
            outs.append(wq[p][q:] + _dot(attns[c, p], blockdiag(vnew)))
            vb = vnew.astype(BF16)
            for i in range(2):
                h = 2 * p + i
                hcol = slice(h * dk, (h + 1) * dk)
                kdt = pre[c]["kd"][:, hcol].T.astype(BF16)
                sts[h] = sts[h] * pre[c]["e_all"][:, hcol] + _dot(kdt, vb[:, i * dk:(i + 1) * dk])
        rows = slice(c * q, (c + 1) * q)
        o = jnp.concatenate(outs, axis=-1)
        o = o * lax.rsqrt(_head_sums(o * o, ones_ref) * (1.0 / dk) + LN_EPS)
        o = o * jnp.concatenate([ng_ref[...]] * GDN_HEADS, axis=-1) * _silu(qkvz_ref[rows, GDN_QKV:].astype(F32))
        out_ref[rows, :] = o.astype(out_ref.dtype)
    for h in range(GDN_HEADS):
        state[h] = sts[h]


def _gdn_branch(qkvz, small, prm, layer, bsz, lp, lr):
    q = CHUNK
    assert q == GDN_HEAD_DIM
    rb = _pick(lp, (3 * q, 2 * q, q))
    nblk = lp // rb
    wx = GDN_QKV + BRANCH_WIDTH
    rowmap = lambda b, r: (b * nblk + r, 0)
    lb = functools.partial(_layer_block, layer=layer)
    shift = _shift_matrix(q)
    e_g = _expand_matrix(SM_GDN_A, GDN_HEADS, GDN_HEAD_DIM, 3)
    e_b = _expand_matrix(SM_GDN_B, GDN_HEADS, GDN_HEAD_DIM, 2)
    ones = _block_ones(2 * LANES, GDN_HEAD_DIM, 1)
    return pl.pallas_call(
        functools.partial(_gdn_kernel, q=q, nc=rb // q, lr=lr),
        grid=(bsz, nblk),
        in_specs=[pl.BlockSpec((rb, wx), rowmap), pl.BlockSpec((rb, LANES), rowmap),
                  _full(shift.shape), lb((CONV_K, GDN_QKV)), lb((1, LANES)), lb((1, LANES)),
                  _full(e_g.shape), _full(e_b.shape), _full(ones.shape), lb((1, GDN_HEAD_DIM))],
        out_specs=pl.BlockSpec((rb, BRANCH_WIDTH), rowmap),
        out_shape=jax.ShapeDtypeStruct((bsz * lp, BRANCH_WIDTH), BF16),
        scratch_shapes=[pltpu.VMEM((rb + CONV_CARRY, GDN_QKV), BF16),
                        pltpu.VMEM((CONV_CARRY, GDN_QKV), BF16),
                        pltpu.VMEM((GDN_HEADS, GDN_HEAD_DIM, GDN_HEAD_DIM), F32)],
        compiler_params=_params("parallel", "arbitrary"),
        name="gdn_branch",
    )(qkvz, small, shift, prm["conv_w"], prm["dt_bias"], prm["neg_a"], e_g, e_b, ones, prm["norm_g"])


def _gdn_prepare(conv_w, dt_bias, a_log, norm_g):
    nl = conv_w.shape[0]
    return dict(conv_w=conv_w, dt_bias=_lane_vectors(dt_bias, SM_GDN_A),
                neg_a=_lane_vectors(-jnp.exp(a_log), SM_GDN_A), norm_g=norm_g.reshape(nl, 1, -1))


def _merge_kernel(h_ref, ya_ref, yb_ref, yc_ref, wg_ref, bg_ref, wbr_ref, wo_ref, g_ref, b_ref,
                  o_ref, ob_ref, *, alpha):
    h = h_ref[...]
    logits = _dot(h.astype(BF16), wg_ref[...]) + bg_ref[...]
    merged = None
    for n, y_ref in enumerate((ya_ref, yb_ref, yc_ref)):
        gate = _sigmoid(logits[:, n * D_MODEL:(n + 1) * D_MODEL])
        term = gate * _dot(y_ref[...], wbr_ref[n])
        merged = term if merged is None else merged + term
    y = alpha * h + _dot(merged.astype(BF16), wo_ref[...])
    y = _ln_math(y, g_ref[...], b_ref[...])
    o_ref[...] = y
    ob_ref[...] = y.astype(BF16)


def _merge(h, ya, yb, yc, prm, layer, alpha):
    m, d = h.shape
    tm = _pick(m, (512, 256, 128))
    rows = lambda w: pl.BlockSpec((tm, w), lambda i: (i, 0))
    lb = functools.partial(_layer_block, layer=layer)
    return pl.pallas_call(
        functools.partial(_merge_kernel, alpha=alpha),
        grid=(m // tm,),
        in_specs=[rows(d), rows(BRANCH_WIDTH), rows(BRANCH_WIDTH), rows(BRANCH_WIDTH),
                  lb((d, N_BRANCH * d)), lb((1, N_BRANCH * d)),
                  lb((N_BRANCH, BRANCH_WIDTH, d)), lb((d, d)), lb((1, d)), lb((1, d))],
        out_specs=[rows(d), rows(d)],
        out_shape=[jax.ShapeDtypeStruct((m, d), F32), jax.ShapeDtypeStruct((m, d), BF16)],
        compiler_params=_params("parallel"),
        name="merge",
    )(h, ya, yb, yc, prm["w_gate"], prm["b_gate"], prm["w_branch"], prm["w_out"], prm["ln_g"], prm["ln_b"])


def _split_in_proj(w):
    widths = (BRANCH_WIDTH, BRANCH_WIDTH, SSD_CONV_WIDTH, SSD_HEADS, BRANCH_WIDTH,
              GDN_QKV, GDN_HEADS, GDN_HEADS, BRANCH_WIDTH, N_BRANCH * D_MODEL)
    offs = [0]
    for wd in widths:
        offs.append(offs[-1] + wd)
    seg = [w[:, :, offs[i]:offs[i + 1]] for i in range(len(widths))]
    s5_u, s5_z, ssd_xbc, ssd_dt, ssd_z, gdn_qkv, gdn_a, gdn_b, gdn_z, gate = seg
    pad = jnp.zeros(w.shape[:2] + (LANES - SSD_HEADS - 2 * GDN_HEADS,), w.dtype)
    return dict(
        s5=jnp.concatenate([s5_u, s5_z], axis=2).astype(BF16),
        ssd=jnp.concatenate([ssd_xbc, ssd_z], axis=2).astype(BF16),
        gdn=jnp.concatenate([gdn_qkv, gdn_z], axis=2).astype(BF16),
        small=jnp.concatenate([ssd_dt, gdn_a, gdn_b, pad], axis=2).astype(BF16),
        gate=gate.astype(BF16))


def kernel(x, meta, ln_in_g, ln_in_b, w_in, s5_a_re, s5_a_im, s5_log_step, s5_b_re, s5_b_im, s5_c_re, s5_c_im, s5_d, s5_w_glu, s5_b_glu, ssd_conv_w, ssd_conv_b, ssd_dt_bias, ssd_a_log, ssd_d, ssd_norm_g, gdn_conv_w, gdn_dt_bias, gdn_a_log, gdn_norm_g, w_branch, b_gate, w_out, ln_g, ln_b):
    bsz, seq, d = x.shape
    depth = w_in.shape[0]
    alpha = (2 * depth) ** 0.25
    lr = N_META + seq
    lp = -(-lr // CHUNK) * CHUNK
    h0 = jnp.concatenate([jnp.broadcast_to(meta[None].astype(x.dtype), (bsz, N_META, d)), x,
                          jnp.zeros((bsz, lp - lr, d), x.dtype)], axis=1).reshape(bsz * lp, d)
    h, hb = _layer_norm_in(h0, ln_in_g, ln_in_b)

    wi = _split_in_proj(w_in)
    s5p = _s5_prepare(s5_a_re, s5_a_im, s5_log_step, s5_b_re, s5_b_im, s5_c_re, s5_c_im, s5_d, s5_w_glu, s5_b_glu)
    ssdp = _ssd_prepare(ssd_conv_w, ssd_conv_b, ssd_dt_bias, ssd_a_log, ssd_d, ssd_norm_g)
    gdnp = _gdn_prepare(gdn_conv_w, gdn_dt_bias, gdn_a_log, gdn_norm_g)
    mrg = dict(w_gate=wi["gate"], b_gate=b_gate.reshape(depth, 1, -1), w_branch=w_branch.astype(BF16),
               w_out=w_out.astype(BF16), ln_g=ln_g.reshape(depth, 1, -1), ln_b=ln_b.reshape(depth, 1, -1))

    for layer in range(depth):
        small = _matmul(hb, wi["small"], layer, F32, "proj_small")
        y_a = _s5_branch(_matmul(hb, wi["s5"], layer, BF16, "proj_s5"), s5p, layer, bsz, lp)
        y_b = _ssd_branch(_matmul(hb, wi["ssd"], layer, BF16, "proj_ssd"), small, ssdp, layer, bsz, lp)
        y_c = _gdn_branch(_matmul(hb, wi["gdn"], layer, BF16, "proj_gdn"), small, gdnp, layer, bsz, lp, lr)
        h, hb = _merge(h, y_a, y_b, y_c, mrg, layer, alpha)

    return h.reshape(bsz, lp, d)[:, N_META:lr]
```

```python
import functools
import math

import numpy as np
import jax
import jax.numpy as jnp
from jax import lax
from jax.experimental import pallas as pl
from jax.experimental.pallas import tpu as pltpu

D_MODEL = 1024
N_META = 16
CONV_K = 4
N_BRANCH = 3
BRANCH_WIDTH = 768

S5_GROUP = 16
S5_GROUPS = BRANCH_WIDTH // S5_GROUP
S5_STATE = 64
S5_STATES = S5_GROUPS * S5_STATE
S5_SLAB_IN = 256
S5_SLAB_ST = 1024
S5_SLABS = BRANCH_WIDTH // S5_SLAB_IN
S5_T = 64
S5_CARRY_ROWS = 16

SSD_HEAD_DIM = 64
SSD_HEADS = 12
SSD_GROUPS = 2
SSD_HPG = SSD_HEADS // SSD_GROUPS
SSD_STATE = 128
SSD_BC = SSD_GROUPS * SSD_STATE
SSD_CONV_WIDTH = BRANCH_WIDTH + 2 * SSD_BC
SSD_GW = SSD_HPG * SSD_HEAD_DIM

GDN_HEAD_DIM = 128
GDN_HEADS = 6
GDN_QKV = 3 * BRANCH_WIDTH

CHUNK = 128
LANES = 128
CONV_CARRY = 16
LN_EPS = 1e-5

SM_SSD_DT = 0
SM_GDN_A = SSD_HEADS
SM_GDN_B = SSD_HEADS + GDN_HEADS

VMEM_LIMIT = 56 * 1024 * 1024

F32 = jnp.float32
BF16 = jnp.bfloat16


def _dot(a, b):
    return jnp.dot(a, b, preferred_element_type=F32)


def _cumsum_rows(tril, x):
    parts = _dot(tril, _split3(x))
    return parts[:, :LANES] + parts[:, LANES:2 * LANES] + parts[:, 2 * LANES:]


def _dot_nt(a, b):
    return lax.dot_general(a, b, (((1,), (1,)), ((), ())), preferred_element_type=F32)


def _sigmoid(x):
    return 1.0 / (1.0 + jnp.exp(-x))


def _silu(x):
    return x * _sigmoid(x)


def _softplus(x):
    return jnp.maximum(x, 0.0) + jnp.log1p(jnp.exp(-jnp.abs(x)))


def _gelu_tanh(x):
    c = math.sqrt(2.0 / math.pi)
    return 0.5 * x * (1.0 + jnp.tanh(c * (x + 0.044715 * (x * x * x))))


def _bf16_floor(x):
    bits = lax.bitcast_convert_type(x, jnp.uint32) & jnp.uint32(0xFFFF0000)
    return lax.bitcast_convert_type(bits, F32)


def _split3(x):
    hi = _bf16_floor(x)
    r1 = x - hi
    mid = _bf16_floor(r1)
    return jnp.concatenate([hi.astype(BF16), mid.astype(BF16), (r1 - mid).astype(BF16)], axis=-1)


def _split2(x):
    hi = _bf16_floor(x)
    return jnp.concatenate([hi.astype(BF16), (x - hi).astype(BF16)], axis=-1)


def _pick(n, candidates):
    for c in candidates:
        if n % c == 0:
            return c
    raise ValueError(f"no tile for {n} in {candidates}")


def _params(*semantics):
    return pltpu.CompilerParams(dimension_semantics=semantics, vmem_limit_bytes=VMEM_LIMIT)


def _full(shape):
    n = len(shape)
    return pl.BlockSpec(shape, lambda *_: (0,) * n)


def _layer_block(shape, layer):
    n = len(shape)
    return pl.BlockSpec((None,) + tuple(shape), lambda *_: (layer,) + (0,) * n)


def _expand_matrix(first_lane, heads, width, pieces):
    e = np.zeros((LANES, heads * width), np.float32)
    for h in range(heads):
        e[first_lane + h, h * width:(h + 1) * width] = 1.0
    return jnp.asarray(np.tile(e, (pieces, 1)), BF16)


def _shift_matrix(q):
    s = np.zeros(((CONV_K - 1) * q, q + CONV_CARRY), np.float32)
    for j in range(CONV_K - 1):
        for t in range(q):
            s[j * q + t, CONV_CARRY + t - (CONV_K - 1) + j] = 1.0
    return jnp.asarray(s, BF16)


def _block_ones(width, block, pieces):
    o = np.kron(np.eye(width // block, dtype=np.float32), np.ones((block, block), np.float32))
    return jnp.asarray(np.tile(o, (pieces, 1)), BF16)


def _ln_math(x, g, b):
    mu = jnp.mean(x, axis=-1, keepdims=True)
    xc = x - mu
    var = jnp.mean(xc * xc, axis=-1, keepdims=True)
    return xc * lax.rsqrt(var + LN_EPS) * g + b


def _ln_kernel(x_ref, g_ref, b_ref, o_ref, ob_ref):
    y = _ln_math(x_ref[...], g_ref[...], b_ref[...])
    o_ref[...] = y
    ob_ref[...] = y.astype(BF16)


def _layer_norm_in(x2d, g, b):
    m, d = x2d.shape
    tm = _pick(m, (512, 256, 128))
    return pl.pallas_call(
        _ln_kernel,
        grid=(m // tm,),
        in_specs=[pl.BlockSpec((tm, d), lambda i: (i, 0)), _full((1, d)), _full((1, d))],
        out_specs=[pl.BlockSpec((tm, d), lambda i: (i, 0)), pl.BlockSpec((tm, d), lambda i: (i, 0))],
        out_shape=[jax.ShapeDtypeStruct((m, d), F32), jax.ShapeDtypeStruct((m, d), BF16)],
        compiler_params=_params("parallel"),
        name="ln_in",
    )(x2d, g.reshape(1, d), b.reshape(1, d))


def _in_proj_kernel(x_ref, *refs):
    n = len(refs) // 2
    x = x_ref[...]
    for w_ref, o_ref in zip(refs[:n], refs[n:]):
        o_ref[...] = _dot(x, w_ref[...]).astype(o_ref.dtype)


def _in_proj(x, weights, layer, out_dtypes):
    m, k = x.shape
    tm = _pick(m, (768, 512, 256, 128))
    widths = [w.shape[-1] for w in weights]
    rows = lambda n: pl.BlockSpec((tm, n), lambda i: (i, 0))
    return pl.pallas_call(
        _in_proj_kernel,
        grid=(m // tm,),
        in_specs=[rows(k)] + [_layer_block((k, n), layer) for n in widths],
        out_specs=[rows(n) for n in widths],
        out_shape=[jax.ShapeDtypeStruct((m, n), dt) for n, dt in zip(widths, out_dtypes)],
        compiler_params=_params("parallel"),
        name="in_proj",
    )(x, *weights)


def _carry_rows(c):
    hi = _bf16_floor(c)
    r1 = c - hi
    mid = _bf16_floor(r1)
    lo = r1 - mid
    rid = lax.broadcasted_iota(jnp.int32, (S5_CARRY_ROWS, c.shape[1]), 0)
    ext = jnp.where(rid == 0, hi, jnp.where(rid == 1, mid, jnp.where(rid == 2, lo, 0.0)))
    return ext.astype(BF16)


def _s5_kernel(uz_ref, wbre_ref, wbim_ref, pnr_ref, pni_ref, ppr_ref, ppi_ref, abr_ref, abi_ref,
               plr_ref, pli_ref, cre_ref, cim_ref, d_ref, wglu_ref, bglu_ref, out_ref,
               bure, buim, sre, sim, st_re, st_im, *, rb, ns):
    @pl.when(pl.program_id(1) == 0)
    def _():
        st_re[...] = jnp.zeros_like(st_re)
        st_im[...] = jnp.zeros_like(st_im)

    streams = range(ns)
    slabs = [slice(k * S5_SLAB_ST, (k + 1) * S5_SLAB_ST) for k in range(S5_SLABS)]
    us = [uz_ref[s, :, :BRANCH_WIDTH] for s in streams]
    for s in streams:
        for k in range(S5_SLABS):
            uk = us[s][:, k * S5_SLAB_IN:(k + 1) * S5_SLAB_IN]
            bure[s, :, slabs[k]] = _dot(uk, wbre_ref[k]).astype(BF16)
            buim[s, :, slabs[k]] = _dot(uk, wbim_ref[k]).astype(BF16)

    t = S5_T
    row = lax.broadcasted_iota(jnp.int32, (t, t + S5_CARRY_ROWS), 0)
    col = lax.broadcasted_iota(jnp.int32, (t, t + S5_CARRY_ROWS), 1)
    ones_cols = jnp.where(col >= t, 1.0, 0.0) * jnp.where(col < t + 3, 1.0, 0.0)
    tril = (jnp.where(row >= col, 1.0, 0.0) + ones_cols).astype(BF16)

    for c in range(rb // t):
        rows = slice(c * t, (c + 1) * t)
        for s in streams:
            for sl in slabs:
                br = bure[s, rows, sl]
                bi = buim[s, rows, sl]
                pnr = pnr_ref[:, sl]
                pni = pni_ref[:, sl]
                s_in_r = st_re[s, :, sl]
                s_in_i = st_im[s, :, sl]
                abr = abr_ref[:, sl]
                abi = abi_ref[:, sl]
                xr = jnp.concatenate([br * pnr - bi * pni, _carry_rows(abr * s_in_r - abi * s_in_i)], axis=0)
                xi = jnp.concatenate([br * pni + bi * pnr, _carry_rows(abr * s_in_i + abi * s_in_r)], axis=0)
                acc_r = _dot(tril, xr)
                acc_i = _dot(tril, xi)
                lr = acc_r[t - 1:t, :]
                li = acc_i[t - 1:t, :]
                plr = plr_ref[:, sl]
                pli = pli_ref[:, sl]
                st_re[s, :, sl] = lr * plr - li * pli
                st_im[s, :, sl] = lr * pli + li * plr
                ar = acc_r.astype(BF16)
                ai = acc_i.astype(BF16)
                ppr = ppr_ref[:, sl]
                ppi = ppi_ref[:, sl]
                sre[s, rows, sl] = ar * ppr - ai * ppi
                sim[s, rows, sl] = ar * ppi + ai * ppr

    for s in streams:
        ys = [_dot(sre[s, :, slabs[k]], cre_ref[k]) - _dot(sim[s, :, slabs[k]], cim_ref[k])
              for k in range(S5_SLABS)]
        y = jnp.concatenate(ys, axis=-1) + d_ref[...] * us[s].astype(F32)
        v = _gelu_tanh(y)
        v = v * _sigmoid(_dot(v.astype(BF16), wglu_ref[...]) + bglu_ref[...])
        z = uz_ref[s, :, BRANCH_WIDTH:].astype(F32)
        out_ref[s] = (v * _silu(z)).astype(out_ref.dtype)


def _s5_branch(uz, prm, layer, bsz, lp):
    rb = _pick(lp, (384, 256, 128))
    ns = _pick(bsz, (2, 1))
    w2 = 2 * BRANCH_WIDTH
    lb = functools.partial(_layer_block, layer=layer)
    tab = lb((S5_T, S5_STATES))
    rowv = lb((1, S5_STATES))
    wb = lb((S5_SLABS, S5_SLAB_IN, S5_SLAB_ST))
    cb = lb((S5_SLABS, S5_SLAB_ST, S5_SLAB_IN))
    vec = lb((1, BRANCH_WIDTH))
    out = pl.pallas_call(
        functools.partial(_s5_kernel, rb=rb, ns=ns),
        grid=(bsz // ns, lp // rb),
        in_specs=[pl.BlockSpec((ns, rb, w2), lambda b, r: (b, r, 0)),
                  wb, wb, tab, tab, tab, tab, rowv, rowv, rowv, rowv, cb, cb, vec,
                  lb((BRANCH_WIDTH, BRANCH_WIDTH)), vec],
        out_specs=pl.BlockSpec((ns, rb, BRANCH_WIDTH), lambda b, r: (b, r, 0)),
        out_shape=jax.ShapeDtypeStruct((bsz, lp, BRANCH_WIDTH), BF16),
        scratch_shapes=[pltpu.VMEM((ns, rb, S5_STATES), BF16), pltpu.VMEM((ns, rb, S5_STATES), BF16),
                        pltpu.VMEM((ns, rb, S5_STATES), BF16), pltpu.VMEM((ns, rb, S5_STATES), BF16),
                        pltpu.VMEM((ns, 1, S5_STATES), F32), pltpu.VMEM((ns, 1, S5_STATES), F32)],
        compiler_params=_params("parallel", "arbitrary"),
        name="s5_branch",
    )(uz.reshape(bsz, lp, w2), prm["wb_re"], prm["wb_im"], prm["pn_re"], prm["pn_im"], prm["pp_re"], prm["pp_im"],
      prm["ab_re"], prm["ab_im"], prm["pl_re"], prm["pl_im"], prm["c_re"], prm["c_im"],
      prm["d"], prm["w_glu"], prm["b_glu"])
    return out.reshape(bsz * lp, BRANCH_WIDTH)


def _s5_prepare(a_re, a_im, log_step, b_re, b_im, c_re, c_im, d, w_glu, b_glu):
    nl = a_re.shape[0]
    lam_re = jnp.minimum(a_re, -1e-4)
    lam_im = a_im
    step = jnp.exp(log_step)[..., None]
    dec = lam_re * step
    ang = lam_im * step
    mag = jnp.exp(dec)
    abar_re, abar_im = mag * jnp.cos(ang), mag * jnp.sin(ang)
    den = lam_re * lam_re + lam_im * lam_im
    nr, ni = abar_re - 1.0, abar_im
    coef_re = (nr * lam_re + ni * lam_im) / den
    coef_im = (ni * lam_re - nr * lam_im) / den
    bbar_re = coef_re[..., None] * b_re - coef_im[..., None] * b_im
    bbar_im = coef_re[..., None] * b_im + coef_im[..., None] * b_re

    gps = S5_SLAB_IN // S5_GROUP
    eye = jnp.eye(gps, dtype=F32)

    def in_slabs(bb):
        w = jnp.einsum('lkgpc,gh->lkgchp', bb.reshape(nl, S5_SLABS, gps, S5_STATE, S5_GROUP), eye)
        return w.reshape(nl, S5_SLABS, S5_SLAB_IN, S5_SLAB_ST).astype(BF16)

    def out_slabs(cc):
        w = jnp.einsum('lkgcp,gh->lkgphc', cc.reshape(nl, S5_SLABS, gps, S5_GROUP, S5_STATE), eye)
        return w.reshape(nl, S5_SLABS, S5_SLAB_ST, S5_SLAB_IN).astype(BF16)

    idx = jnp.arange(S5_T, dtype=F32)[None, :, None]
    dec = dec.reshape(nl, 1, S5_STATES)
    ang = ang.reshape(nl, 1, S5_STATES)
    cs, sn = jnp.cos(idx * ang), jnp.sin(idx * ang)
    grow, shrink = jnp.exp(idx * dec), jnp.exp(-idx * dec)
    pp_re, pp_im = grow * cs, grow * sn
    return dict(wb_re=in_slabs(bbar_re), wb_im=in_slabs(bbar_im),
                pn_re=(shrink * cs).astype(BF16), pn_im=(-shrink * sn).astype(BF16),
                pp_re=pp_re.astype(BF16), pp_im=pp_im.astype(BF16),
                ab_re=abar_re.reshape(nl, 1, S5_STATES), ab_im=abar_im.reshape(nl, 1, S5_STATES),
                pl_re=pp_re[:, S5_T - 1:], pl_im=pp_im[:, S5_T - 1:],
                c_re=out_slabs(c_re), c_im=out_slabs(c_im), d=d.reshape(nl, 1, -1),
                w_glu=w_glu.astype(BF16), b_glu=b_glu.reshape(nl, 1, -1))


def _conv_begin(buf, carry, x_ref, width, first):
    rb = x_ref.shape[0]

    @pl.when(first)
    def _():
        carry[...] = jnp.zeros_like(carry)

    buf[pl.ds(0, CONV_CARRY), :] = carry[...]
    buf[pl.ds(CONV_CARRY, rb), :] = x_ref[:, :width]
    carry[...] = x_ref[rb - CONV_CARRY:rb, :width]


def _causal_conv(buf, c, x, shift_ref, w_ref, q):
    taps = _dot(shift_ref[...], buf[pl.ds(c * q, q + CONV_CARRY), :])
    acc = x.astype(F32) * w_ref[CONV_K - 1:CONV_K, :]
    for j in range(CONV_K - 1):
        acc = acc + taps[j * q:(j + 1) * q, :] * w_ref[j:j + 1, :]
    return acc


def _tri_masks(q):
    row = lax.broadcasted_iota(jnp.int32, (q, q), 0)
    col = lax.broadcasted_iota(jnp.int32, (q, q), 1)
    return row >= col, row > col, row == col


def _ssd_kernel(xz_ref, sm_ref, shift_ref, cw_ref, cb_ref, dtb_ref, nega_ref, e64_ref, e128_ref,
                d_ref, ng_ref, out_ref, buf, carry, state, *, q, nc):
    first = pl.program_id(1) == 0

    @pl.when(first)
    def _():
        state[...] = jnp.zeros_like(state)

    _conv_begin(buf, carry, xz_ref, SSD_CONV_WIDTH, first)

    causal, _, _ = _tri_masks(q)
    tril = jnp.where(causal, 1.0, 0.0).astype(BF16)
    low_half = lax.broadcasted_iota(jnp.int32, (q, LANES), 1) < SSD_HEAD_DIM

    pre = []
    for c in range(nc):
        rows = slice(c * q, (c + 1) * q)
        xbc = _silu(_causal_conv(buf, c, xz_ref[rows, :SSD_CONV_WIDTH], shift_ref, cw_ref, q) + cb_ref[...])
        xs = xbc[:, :BRANCH_WIDTH]
        bmat = xbc[:, BRANCH_WIDTH:BRANCH_WIDTH + SSD_BC]
        cmat = xbc[:, BRANCH_WIDTH + SSD_BC:]
        dt = _softplus(sm_ref[rows, :] + dtb_ref[...])
        da = dt * nega_ref[...]
        acum = _cumsum_rows(tril, da)
        acum_t = acum.T
        a3 = _split3(acum)
        acum64 = _dot(a3, e64_ref[...])
        acum128 = _dot(a3, e128_ref[...])
        dt64 = _dot(_split3(dt), e64_ref[...])
        alast = acum64[q - 1:q, :]
        xd = xs * dt64
        bgs = [bmat[:, g * SSD_STATE:(g + 1) * SSD_STATE] for g in range(SSD_GROUPS)]
        cgs = [cmat[:, g * SSD_STATE:(g + 1) * SSD_STATE].astype(BF16) for g in range(SSD_GROUPS)]
        scores = [_dot_nt(cgs[g], bgs[g].astype(BF16)) for g in range(SSD_GROUPS)]
        ms, rhs = [], []
        for h in range(SSD_HEADS):
            lmat = jnp.where(causal, jnp.exp(acum128[:, h * LANES:(h + 1) * LANES] - acum_t[h:h + 1, :]), 0.0)
            ms.append((scores[h // SSD_HPG] * lmat).astype(BF16))
            pair = xd[:, (h // 2) * LANES:(h // 2 + 1) * LANES]
            keep = low_half if h % 2 == 0 else jnp.logical_not(low_half)
            rhs.append(jnp.where(keep, pair, 0.0).astype(BF16))
        ypairs = [_dot(ms[2 * p], rhs[2 * p]) + _dot(ms[2 * p + 1], rhs[2 * p + 1])
                  for p in range(SSD_HEADS // 2)]
        pre.append(dict(
            y=jnp.concatenate(ypairs, axis=-1) + xs * d_ref[...], cgs=cgs,
            bts=[bgs[g].T.astype(BF16) for g in range(SSD_GROUPS)],
            e_in=jnp.exp(acum64),
            xdw=(xd * jnp.exp(alast - acum64)).astype(BF16),
            e_all=jnp.exp(alast)))

    sts = [state[g] for g in range(SSD_GROUPS)]
    for c in range(nc):
        rows = slice(c * q, (c + 1) * q)
        pc = pre[c]
        y_off = jnp.concatenate([_dot(pc["cgs"][g], sts[g].astype(BF16)) for g in range(SSD_GROUPS)], axis=-1)
        for g in range(SSD_GROUPS):
            gs = slice(g * SSD_GW, (g + 1) * SSD_GW)
            sts[g] = sts[g] * pc["e_all"][:, gs] + _dot(pc["bts"][g], pc["xdw"][:, gs])
        y = (pc["y"] + y_off * pc["e_in"]) * _silu(xz_ref[rows, SSD_CONV_WIDTH:].astype(F32))
        y = y * lax.rsqrt(jnp.mean(y * y, axis=-1, keepdims=True) + LN_EPS) * ng_ref[...]
        out_ref[rows, :] = y.astype(out_ref.dtype)
    for g in range(SSD_GROUPS):
        state[g] = sts[g]


def _ssd_branch(xz, small, prm, layer, bsz, lp):
    q = CHUNK
    rb = _pick(lp, (3 * q, 2 * q, q))
    nblk = lp // rb
    wx = SSD_CONV_WIDTH + BRANCH_WIDTH
    rowmap = lambda b, r: (b * nblk + r, 0)
    lb = functools.partial(_layer_block, layer=layer)
    shift = _shift_matrix(q)
    e64 = _expand_matrix(SM_SSD_DT, SSD_HEADS, SSD_HEAD_DIM, 3)
    e128 = _expand_matrix(SM_SSD_DT, SSD_HEADS, LANES, 3)
    return pl.pallas_call(
        functools.partial(_ssd_kernel, q=q, nc=rb // q),
        grid=(bsz, nblk),
        in_specs=[pl.BlockSpec((rb, wx), rowmap), pl.BlockSpec((rb, LANES), rowmap),
                  _full(shift.shape), lb((CONV_K, SSD_CONV_WIDTH)), lb((1, SSD_CONV_WIDTH)),
                  lb((1, LANES)), lb((1, LANES)), _full(e64.shape), _full(e128.shape),
                  lb((1, BRANCH_WIDTH)), lb((1, BRANCH_WIDTH))],
        out_specs=pl.BlockSpec((rb, BRANCH_WIDTH), rowmap),
        out_shape=jax.ShapeDtypeStruct((bsz * lp, BRANCH_WIDTH), BF16),
        scratch_shapes=[pltpu.VMEM((rb + CONV_CARRY, SSD_CONV_WIDTH), BF16),
                        pltpu.VMEM((CONV_CARRY, SSD_CONV_WIDTH), BF16),
                        pltpu.VMEM((SSD_GROUPS, SSD_STATE, SSD_GW), F32)],
        compiler_params=_params("parallel", "arbitrary"),
        name="ssd_branch",
    )(xz, small, shift, prm["conv_w"], prm["conv_b"], prm["dt_bias"], prm["neg_a"], e64, e128,
      prm["d"], prm["norm_g"])


def _lane_vectors(values, offset):
    nl, n = values.shape
    return jnp.pad(values.astype(F32), ((0, 0), (offset, LANES - offset - n))).reshape(nl, 1, LANES)


def _ssd_prepare(conv_w, conv_b, dt_bias, a_log, d, norm_g):
    nl = conv_w.shape[0]
    return dict(conv_w=conv_w, conv_b=conv_b.reshape(nl, 1, -1),
                dt_bias=_lane_vectors(dt_bias, SM_SSD_DT),
                neg_a=_lane_vectors(-jnp.exp(a_log), SM_SSD_DT),
                d=jnp.repeat(d, SSD_HEAD_DIM, axis=1).reshape(nl, 1, -1), norm_g=norm_g.reshape(nl, 1, -1))


def _head_sums(x, ones_ref):
    outs = []
    for s in range(x.shape[1] // (2 * LANES)):
        xs = x[:, s * 2 * LANES:(s + 1) * 2 * LANES]
        outs.append(_dot(xs.astype(BF16), ones_ref[...]))
    return jnp.concatenate(outs, axis=-1)


def _gdn_kernel(qkvz_ref, sm_ref, shift_ref, cw_ref, dtb_ref, nega_ref, eg_ref, eb_ref, ones_ref, ng_ref,
                out_ref, buf, carry, state, *, q, nc, ns, lr):
    first = pl.program_id(1) == 0

    @pl.when(first)
    def _():
        state[...] = jnp.zeros_like(state)

    for s in range(ns):
        _conv_begin(buf.at[s], carry.at[s], qkvz_ref.at[s], GDN_QKV, first)

    causal, strict, diag = _tri_masks(q)
    tril = jnp.where(causal, 1.0, 0.0).astype(BF16)
    eye = jnp.where(diag, 1.0, 0.0)

    pre = {}
    for s, c in [(s, c) for s in range(ns) for c in range(nc)]:
        rows = slice(c * q, (c + 1) * q)
        qkv = _silu(_causal_conv(buf.at[s], c, qkvz_ref[s, rows, :GDN_QKV], shift_ref, cw_ref, q))
        qa = qkv[:, :BRANCH_WIDTH]
        ka = qkv[:, BRANCH_WIDTH:2 * BRANCH_WIDTH]
        va = qkv[:, 2 * BRANCH_WIDTH:]
        qa = qa * (lax.rsqrt(_head_sums(qa * qa, ones_ref) + 1e-6) * (GDN_HEAD_DIM ** -0.5))
        ka = ka * lax.rsqrt(_head_sums(ka * ka, ones_ref) + 1e-6)
        sm = sm_ref[s, rows, :]
        glog = nega_ref[...] * _softplus(sm + dtb_ref[...])
        gcum = _cumsum_rows(tril, glog)
        g128 = _dot(_split3(gcum), eg_ref[...])
        beta = _dot(_split2(_sigmoid(sm)), eb_ref[...])
        rid = lax.broadcasted_iota(jnp.int32, beta.shape, 0) + (pl.program_id(1) * (nc * q) + c * q)
        beta = jnp.where(rid < lr, beta, 0.0)
        glast = g128[q - 1:q, :]
        e_in = jnp.exp(g128)
        pre[s, c] = dict(qa=qa, ka=ka, beta=beta, g128=g128, gcum_t=gcum.T, e_all=jnp.exp(glast),
                         vbeta=va * beta, kbe=ka * (beta * e_in), qg=qa * e_in, kd=ka * jnp.exp(glast - g128))

    dk = GDN_HEAD_DIM
    npair = GDN_HEADS // 2
    units = [(s, c, p) for s in range(ns) for c in range(nc) for p in range(npair)]
    ps = [slice(p * 2 * dk, (p + 1) * 2 * dk) for p in range(npair)]
    left = lax.broadcasted_iota(jnp.int32, (q, 2 * dk), 1) < dk

    def blockdiag(x):
        return jnp.concatenate([jnp.where(left, x, 0.0), jnp.where(left, 0.0, x)], axis=0).astype(BF16)

    causal2 = jnp.concatenate([causal, causal], axis=1)
    strict2 = jnp.concatenate([strict, strict], axis=1)
    eye2 = jnp.concatenate([eye, eye], axis=1)
    gammas, pws, tinvs, attns = {}, {}, {}, {}
    for s, c, p in units:
        la = SM_GDN_A + 2 * p
        gt = pre[s, c]["gcum_t"]
        grow = jnp.concatenate([gt[la:la + 1, :], gt[la + 1:la + 2, :]], axis=1)
        gammas[s, c, p] = jnp.where(causal2, jnp.exp(pre[s, c]["g128"][:, ps[p]] - grow), 0.0)
    for s, c, p in units:
        kp = pre[s, c]["ka"][:, ps[p]]
        lhs = jnp.concatenate([kp.astype(BF16), pre[s, c]["qa"][:, ps[p]].astype(BF16)], axis=0)
        kq = _dot_nt(lhs, blockdiag(kp))
        pws[s, c, p] = jnp.where(strict2, -(kq[:q] * gammas[s, c, p] * pre[s, c]["beta"][:, ps[p]]), 0.0)
        attns[s, c, p] = (kq[q:] * gammas[s, c, p]).astype(BF16)
    for u in units:
        tinvs[u] = eye2 + pws[u]
        pws[u] = _dot(pws[u].astype(BF16), blockdiag(pws[u]))
    steps = int(math.log2(q)) - 1
    for i in range(steps):
        last = i == steps - 1
        for u in units:
            pwb = pws[u].astype(BF16)
            lhs = tinvs[u].astype(BF16) if last else jnp.concatenate([tinvs[u].astype(BF16), pwb], axis=0)
            both = _dot(lhs, blockdiag(pws[u]))
            tinvs[u] = tinvs[u] + both[:q]
            if not last:
                pws[u] = both[q:]
    us, ws = {}, {}
    for s, c, p in units:
        uw = []
        for i in range(2):
            hcol = slice((2 * p + i) * dk, (2 * p + i + 1) * dk)
            rhs = jnp.concatenate([pre[s, c]["vbeta"][:, hcol], pre[s, c]["kbe"][:, hcol]], axis=-1)
            uw.append(_dot(tinvs[s, c, p][:, i * dk:(i + 1) * dk].astype(BF16), rhs.astype(BF16)))
        us[s, c, p] = jnp.concatenate([uw[0][:, :dk], uw[1][:, :dk]], axis=-1)
        ws[s, c, p] = jnp.concatenate([uw[0][:, dk:], uw[1][:, dk:]], axis=-1)

    sts = {(s, h): state[s, h] for s in range(ns) for h in range(GDN_HEADS)}
    zero = jnp.zeros((dk, dk), F32)
    for c in range(nc):
        rows = slice(c * q, (c + 1) * q)
        wq = {}
        for s in range(ns):
            for p in range(npair):
                sbd = jnp.concatenate([jnp.concatenate([sts[s, 2 * p], zero], axis=1),
                                       jnp.concatenate([zero, sts[s, 2 * p + 1]], axis=1)], axis=0)
                lhs = jnp.concatenate([ws[s, c, p].astype(BF16), pre[s, c]["qg"][:, ps[p]].astype(BF16)], axis=0)
                wq[s, p] = _dot(lhs, sbd.astype(BF16))
        for s in range(ns):
            outs = []
            for p in range(npair):
                vnew = us[s, c, p] - wq[s, p][:q]
                outs.append(wq[s, p][q:] + _dot(attns[s, c, p], blockdiag(vnew)))
                vb = vnew.astype(BF16)
                for i in range(2):
                    h = 2 * p + i
                    hcol = slice(h * dk, (h + 1) * dk)
                    kdt = pre[s, c]["kd"][:, hcol].T.astype(BF16)
                    sts[s, h] = sts[s, h] * pre[s, c]["e_all"][:, hcol] + _dot(kdt, vb[:, i * dk:(i + 1) * dk])
            o = jnp.concatenate(outs, axis=-1)
            o = o * lax.rsqrt(_head_sums(o * o, ones_ref) * (1.0 / dk) + LN_EPS)
            gate = _silu(qkvz_ref[s, rows, GDN_QKV:].astype(F32))
            o = o * jnp.concatenate([ng_ref[...]] * GDN_HEADS, axis=-1) * gate
            out_ref[s, rows, :] = o.astype(out_ref.dtype)
    for s in range(ns):
        for h in range(GDN_HEADS):
            state[s, h] = sts[s, h]


def _gdn_branch(qkvz, small, prm, layer, bsz, lp, lr):
    q = CHUNK
    assert q == GDN_HEAD_DIM
    rb = _pick(lp, (3 * q, 2 * q, q))
    ns = 1
    wx = GDN_QKV + BRANCH_WIDTH
    rowmap = lambda b, r: (b, r, 0)
    lb = functools.partial(_layer_block, layer=layer)
    shift = _shift_matrix(q)
    e_g = _expand_matrix(SM_GDN_A, GDN_HEADS, GDN_HEAD_DIM, 3)
    e_b = _expand_matrix(SM_GDN_B, GDN_HEADS, GDN_HEAD_DIM, 2)
    ones = _block_ones(2 * LANES, GDN_HEAD_DIM, 1)
    out = pl.pallas_call(
        functools.partial(_gdn_kernel, q=q, nc=rb // q, ns=ns, lr=lr),
        grid=(bsz // ns, lp // rb),
        in_specs=[pl.BlockSpec((ns, rb, wx), rowmap), pl.BlockSpec((ns, rb, LANES), rowmap),
                  _full(shift.shape), lb((CONV_K, GDN_QKV)), lb((1, LANES)), lb((1, LANES)),
                  _full(e_g.shape), _full(e_b.shape), _full(ones.shape), lb((1, GDN_HEAD_DIM))],
        out_specs=pl.BlockSpec((ns, rb, BRANCH_WIDTH), rowmap),
        out_shape=jax.ShapeDtypeStruct((bsz, lp, BRANCH_WIDTH), BF16),
        scratch_shapes=[pltpu.VMEM((ns, rb + CONV_CARRY, GDN_QKV), BF16),
                        pltpu.VMEM((ns, CONV_CARRY, GDN_QKV), BF16),
                        pltpu.VMEM((ns, GDN_HEADS, GDN_HEAD_DIM, GDN_HEAD_DIM), F32)],
        compiler_params=_params("parallel", "arbitrary"),
        name="gdn_branch",
    )(qkvz.reshape(bsz, lp, wx), small.reshape(bsz, lp, LANES), shift, prm["conv_w"], prm["dt_bias"],
      prm["neg_a"], e_g, e_b, ones, prm["norm_g"])
    return out.reshape(bsz * lp, BRANCH_WIDTH)


def _gdn_prepare(conv_w, dt_bias, a_log, norm_g):
    nl = conv_w.shape[0]
    return dict(conv_w=conv_w, dt_bias=_lane_vectors(dt_bias, SM_GDN_A),
                neg_a=_lane_vectors(-jnp.exp(a_log), SM_GDN_A), norm_g=norm_g.reshape(nl, 1, -1))


def _merge_kernel(h_ref, ya_ref, yb_ref, yc_ref, wg_ref, bg_ref, wbr_ref, wo_ref, g_ref, b_ref,
                  o_ref, ob_ref, *, alpha):
    h = h_ref[...]
    logits = _dot(h.astype(BF16), wg_ref[...]) + bg_ref[...]
    merged = None
    for n, y_ref in enumerate((ya_ref, yb_ref, yc_ref)):
        gate = _sigmoid(logits[:, n * D_MODEL:(n + 1) * D_MODEL])
        term = gate * _dot(y_ref[...], wbr_ref[n])
        merged = term if merged is None else merged + term
    y = alpha * h + _dot(merged.astype(BF16), wo_ref[...])
    y = _ln_math(y, g_ref[...], b_ref[...])
    o_ref[...] = y
    ob_ref[...] = y.astype(BF16)


def _merge(h, ya, yb, yc, prm, layer, alpha):
    m, d = h.shape
    tm = _pick(m, (512, 256, 128))
    rows = lambda w: pl.BlockSpec((tm, w), lambda i: (i, 0))
    lb = functools.partial(_layer_block, layer=layer)
    return pl.pallas_call(
        functools.partial(_merge_kernel, alpha=alpha),
        grid=(m // tm,),
        in_specs=[rows(d), rows(BRANCH_WIDTH), rows(BRANCH_WIDTH), rows(BRANCH_WIDTH),
                  lb((d, N_BRANCH * d)), lb((1, N_BRANCH * d)),
                  lb((N_BRANCH, BRANCH_WIDTH, d)), lb((d, d)), lb((1, d)), lb((1, d))],
        out_specs=[rows(d), rows(d)],
        out_shape=[jax.ShapeDtypeStruct((m, d), F32), jax.ShapeDtypeStruct((m, d), BF16)],
        compiler_params=_params("parallel"),
        name="merge",
    )(h, ya, yb, yc, prm["w_gate"], prm["b_gate"], prm["w_branch"], prm["w_out"], prm["ln_g"], prm["ln_b"])


def _split_in_proj(w):
    widths = (BRANCH_WIDTH, BRANCH_WIDTH, SSD_CONV_WIDTH, SSD_HEADS, BRANCH_WIDTH,
              GDN_QKV, GDN_HEADS, GDN_HEADS, BRANCH_WIDTH, N_BRANCH * D_MODEL)
    offs = [0]
    for wd in widths:
        offs.append(offs[-1] + wd)
    seg = [w[:, :, offs[i]:offs[i + 1]] for i in range(len(widths))]
    s5_u, s5_z, ssd_xbc, ssd_dt, ssd_z, gdn_qkv, gdn_a, gdn_b, gdn_z, gate = seg
    pad = jnp.zeros(w.shape[:2] + (LANES - SSD_HEADS - 2 * GDN_HEADS,), w.dtype)
    return dict(
        s5=jnp.concatenate([s5_u, s5_z], axis=2).astype(BF16),
        ssd=jnp.concatenate([ssd_xbc, ssd_z], axis=2).astype(BF16),
        gdn=jnp.concatenate([gdn_qkv, gdn_z], axis=2).astype(BF16),
        small=jnp.concatenate([ssd_dt, gdn_a, gdn_b, pad], axis=2).astype(BF16),
        gate=gate.astype(BF16))


def kernel(x, meta, ln_in_g, ln_in_b, w_in, s5_a_re, s5_a_im, s5_log_step, s5_b_re, s5_b_im, s5_c_re, s5_c_im, s5_d, s5_w_glu, s5_b_glu, ssd_conv_w, ssd_conv_b, ssd_dt_bias, ssd_a_log, ssd_d, ssd_norm_g, gdn_conv_w, gdn_dt_bias, gdn_a_log, gdn_norm_g, w_branch, b_gate, w_out, ln_g, ln_b):
    bsz, seq, d = x.shape
    depth = w_in.shape[0]
    alpha = (2 * depth) ** 0.25
    lr = N_META + seq
    lp = -(-lr // CHUNK) * CHUNK
    h0 = jnp.concatenate([jnp.broadcast_to(meta[None].astype(x.dtype), (bsz, N_META, d)), x,
                          jnp.zeros((bsz, lp - lr, d), x.dtype)], axis=1).reshape(bsz * lp, d)
    h, hb = _layer_norm_in(h0, ln_in_g, ln_in_b)

    wi = _split_in_proj(w_in)
    s5p = _s5_prepare(s5_a_re, s5_a_im, s5_log_step, s5_b_re, s5_b_im, s5_c_re, s5_c_im, s5_d, s5_w_glu, s5_b_glu)
    ssdp = _ssd_prepare(ssd_conv_w, ssd_conv_b, ssd_dt_bias, ssd_a_log, ssd_d, ssd_norm_g)
    gdnp = _gdn_prepare(gdn_conv_w, gdn_dt_bias, gdn_a_log, gdn_norm_g)
    mrg = dict(w_gate=wi["gate"], b_gate=b_gate.reshape(depth, 1, -1), w_branch=w_branch.astype(BF16),
               w_out=w_out.astype(BF16), ln_g=ln_g.reshape(depth, 1, -1), ln_b=ln_b.reshape(depth, 1, -1))

    for layer in range(depth):
        p_s5, p_ssd, p_gdn, small = _in_proj(hb, [wi["s5"], wi["ssd"], wi["gdn"], wi["small"]], layer,
                                             [BF16, BF16, BF16, F32])
        y_a = _s5_branch(p_s5, s5p, layer, bsz, lp)
        y_b = _ssd_branch(p_ssd, small, ssdp, layer, bsz, lp)
        y_c = _gdn_branch(p_gdn, small, gdnp, layer, bsz, lp, lr)
        h, hb = _merge(h, y_a, y_b, y_c, mrg, layer, alpha)

    return h.reshape(bsz, lp, d)[:, N_META:lr]
```

```python
import functools
import math

import numpy as np
import jax
import jax.numpy as jnp
from jax import lax
from jax.experimental import pallas as pl
from jax.experimental.pallas import tpu as pltpu

D_MODEL = 1024
N_META = 16
CONV_K = 4
N_BRANCH = 3
BRANCH_WIDTH = 768

S5_GROUP = 16
S5_GROUPS = BRANCH_WIDTH // S5_GROUP
S5_STATE = 64
S5_STATES = S5_GROUPS * S5_STATE
S5_SLAB_IN = 256
S5_SLAB_ST = 1024
S5_SLABS = BRANCH_WIDTH // S5_SLAB_IN
S5_T = 64
S5_CARRY_ROWS = 16

SSD_HEAD_DIM = 64
SSD_HEADS = 12
SSD_GROUPS = 2
SSD_HPG = SSD_HEADS // SSD_GROUPS
SSD_STATE = 128
SSD_BC = SSD_GROUPS * SSD_STATE
SSD_CONV_WIDTH = BRANCH_WIDTH + 2 * SSD_BC
SSD_GW = SSD_HPG * SSD_HEAD_DIM

GDN_HEAD_DIM = 128
GDN_HEADS = 6
GDN_QKV = 3 * BRANCH_WIDTH
GDN_INV_BASE = 64

CHUNK = 128
LANES = 128
CONV_CARRY = 16
LN_EPS = 1e-5

SM_SSD_DT = 0
SM_GDN_A = SSD_HEADS
SM_GDN_B = SSD_HEADS + GDN_HEADS

VMEM_LIMIT = 56 * 1024 * 1024

F32 = jnp.float32
BF16 = jnp.bfloat16


def _dot(a, b):
    return jnp.dot(a, b, preferred_element_type=F32)


def _cumsum_rows(tril, x):
    parts = _dot(tril, _split3(x))
    return parts[:, :LANES] + parts[:, LANES:2 * LANES] + parts[:, 2 * LANES:]


def _dot_nt(a, b):
    return lax.dot_general(a, b, (((1,), (1,)), ((), ())), preferred_element_type=F32)


def _sigmoid(x):
    return 1.0 / (1.0 + jnp.exp(-x))


def _silu(x):
    return x * _sigmoid(x)


def _softplus(x):
    return jnp.maximum(x, 0.0) + jnp.log1p(jnp.exp(-jnp.abs(x)))


def _gelu_tanh(x):
    c = math.sqrt(2.0 / math.pi)
    return 0.5 * x * (1.0 + jnp.tanh(c * (x + 0.044715 * (x * x * x))))


def _bf16_floor(x):
    bits = lax.bitcast_convert_type(x, jnp.uint32) & jnp.uint32(0xFFFF0000)
    return lax.bitcast_convert_type(bits, F32)


def _split3(x):
    hi = _bf16_floor(x)
    r1 = x - hi
    mid = _bf16_floor(r1)
    return jnp.concatenate([hi.astype(BF16), mid.astype(BF16), (r1 - mid).astype(BF16)], axis=-1)


def _split2(x):
    hi = _bf16_floor(x)
    return jnp.concatenate([hi.astype(BF16), (x - hi).astype(BF16)], axis=-1)


def _pick(n, candidates):
    for c in candidates:
        if n % c == 0:
            return c
    raise ValueError(f"no tile for {n} in {candidates}")


def _params(*semantics):
    return pltpu.CompilerParams(dimension_semantics=semantics, vmem_limit_bytes=VMEM_LIMIT)


def _full(shape):
    n = len(shape)
    return pl.BlockSpec(shape, lambda *_: (0,) * n)


def _layer_block(shape, layer):
    n = len(shape)
    return pl.BlockSpec((None,) + tuple(shape), lambda *_: (layer,) + (0,) * n)


def _expand_matrix(first_lane, heads, width, pieces):
    e = np.zeros((LANES, heads * width), np.float32)
    for h in range(heads):
        e[first_lane + h, h * width:(h + 1) * width] = 1.0
    return jnp.asarray(np.tile(e, (pieces, 1)), BF16)


def _shift_matrix(q):
    s = np.zeros(((CONV_K - 1) * q, q + CONV_CARRY), np.float32)
    for j in range(CONV_K - 1):
        for t in range(q):
            s[j * q + t, CONV_CARRY + t - (CONV_K - 1) + j] = 1.0
    return jnp.asarray(s, BF16)


def _block_ones(width, block, pieces):
    o = np.kron(np.eye(width // block, dtype=np.float32), np.ones((block, block), np.float32))
    return jnp.asarray(np.tile(o, (pieces, 1)), BF16)


def _ln_math(x, g, b):
    mu = jnp.mean(x, axis=-1, keepdims=True)
    xc = x - mu
    var = jnp.mean(xc * xc, axis=-1, keepdims=True)
    return xc * lax.rsqrt(var + LN_EPS) * g + b


def _ln_kernel(x_ref, g_ref, b_ref, o_ref, ob_ref):
    y = _ln_math(x_ref[...], g_ref[...], b_ref[...])
    o_ref[...] = y
    ob_ref[...] = y.astype(BF16)


def _layer_norm_in(x2d, g, b):
    m, d = x2d.shape
    tm = _pick(m, (512, 256, 128))
    return pl.pallas_call(
        _ln_kernel,
        grid=(m // tm,),
        in_specs=[pl.BlockSpec((tm, d), lambda i: (i, 0)), _full((1, d)), _full((1, d))],
        out_specs=[pl.BlockSpec((tm, d), lambda i: (i, 0)), pl.BlockSpec((tm, d), lambda i: (i, 0))],
        out_shape=[jax.ShapeDtypeStruct((m, d), F32), jax.ShapeDtypeStruct((m, d), BF16)],
        compiler_params=_params("parallel"),
        name="ln_in",
    )(x2d, g.reshape(1, d), b.reshape(1, d))


def _in_proj_kernel(x_ref, *refs):
    n = len(refs) // 2
    x = x_ref[...]
    for w_ref, o_ref in zip(refs[:n], refs[n:]):
        o_ref[...] = _dot(x, w_ref[...]).astype(o_ref.dtype)


def _in_proj(x, weights, layer, out_dtypes):
    m, k = x.shape
    tm = _pick(m, (768, 512, 256, 128))
    widths = [w.shape[-1] for w in weights]
    rows = lambda n: pl.BlockSpec((tm, n), lambda i: (i, 0))
    return pl.pallas_call(
        _in_proj_kernel,
        grid=(m // tm,),
        in_specs=[rows(k)] + [_layer_block((k, n), layer) for n in widths],
        out_specs=[rows(n) for n in widths],
        out_shape=[jax.ShapeDtypeStruct((m, n), dt) for n, dt in zip(widths, out_dtypes)],
        compiler_params=_params("parallel"),
        name="in_proj",
    )(x, *weights)


def _carry_rows(c):
    hi = _bf16_floor(c)
    r1 = c - hi
    mid = _bf16_floor(r1)
    lo = r1 - mid
    rid = lax.broadcasted_iota(jnp.int32, (S5_CARRY_ROWS, c.shape[1]), 0)
    ext = jnp.where(rid == 0, hi, jnp.where(rid == 1, mid, jnp.where(rid == 2, lo, 0.0)))
    return ext.astype(BF16)


def _s5_kernel(uz_ref, wbre_ref, wbim_ref, pnr_ref, pni_ref, ppr_ref, ppi_ref, abr_ref, abi_ref,
               plr_ref, pli_ref, cre_ref, cim_ref, d_ref, wglu_ref, bglu_ref, out_ref,
               bure, buim, sre, sim, st_re, st_im, *, rb, ns):
    @pl.when(pl.program_id(1) == 0)
    def _():
        st_re[...] = jnp.zeros_like(st_re)
        st_im[...] = jnp.zeros_like(st_im)

    streams = range(ns)
    slabs = [slice(k * S5_SLAB_ST, (k + 1) * S5_SLAB_ST) for k in range(S5_SLABS)]
    us = [uz_ref[s, :, :BRANCH_WIDTH] for s in streams]
    for s in streams:
        for k in range(S5_SLABS):
            uk = us[s][:, k * S5_SLAB_IN:(k + 1) * S5_SLAB_IN]
            bure[s, :, slabs[k]] = _dot(uk, wbre_ref[k]).astype(BF16)
            buim[s, :, slabs[k]] = _dot(uk, wbim_ref[k]).astype(BF16)

    t = S5_T
    row = lax.broadcasted_iota(jnp.int32, (t, t + S5_CARRY_ROWS), 0)
    col = lax.broadcasted_iota(jnp.int32, (t, t + S5_CARRY_ROWS), 1)
    ones_cols = jnp.where(col >= t, 1.0, 0.0) * jnp.where(col < t + 3, 1.0, 0.0)
    tril = (jnp.where(row >= col, 1.0, 0.0) + ones_cols).astype(BF16)

    for c in range(rb // t):
        rows = slice(c * t, (c + 1) * t)
        for s in streams:
            for sl in slabs:
                br = bure[s, rows, sl]
                bi = buim[s, rows, sl]
                pnr = pnr_ref[:, sl]
                pni = pni_ref[:, sl]
                s_in_r = st_re[s, :, sl]
                s_in_i = st_im[s, :, sl]
                abr = abr_ref[:, sl]
                abi = abi_ref[:, sl]
                xr = jnp.concatenate([br * pnr - bi * pni, _carry_rows(abr * s_in_r - abi * s_in_i)], axis=0)
                xi = jnp.concatenate([br * pni + bi * pnr, _carry_rows(abr * s_in_i + abi * s_in_r)], axis=0)
                acc_r = _dot(tril, xr)
                acc_i = _dot(tril, xi)
                lr = acc_r[t - 1:t, :]
                li = acc_i[t - 1:t, :]
                plr = plr_ref[:, sl]
                pli = pli_ref[:, sl]
                st_re[s, :, sl] = lr * plr - li * pli
                st_im[s, :, sl] = lr * pli + li * plr
                ar = acc_r.astype(BF16)
                ai = acc_i.astype(BF16)
                ppr = ppr_ref[:, sl]
                ppi = ppi_ref[:, sl]
                sre[s, rows, sl] = ar * ppr - ai * ppi
                sim[s, rows, sl] = ar * ppi + ai * ppr

    for s in streams:
        ys = [_dot(sre[s, :, slabs[k]], cre_ref[k]) - _dot(sim[s, :, slabs[k]], cim_ref[k])
              for k in range(S5_SLABS)]
        y = jnp.concatenate(ys, axis=-1) + d_ref[...] * us[s].astype(F32)
        v = _gelu_tanh(y)
        v = v * _sigmoid(_dot(v.astype(BF16), wglu_ref[...]) + bglu_ref[...])
        z = uz_ref[s, :, BRANCH_WIDTH:].astype(F32)
        out_ref[s] = (v * _silu(z)).astype(out_ref.dtype)


def _s5_branch(uz, prm, layer, bsz, lp):
    rb = _pick(lp, (384, 256, 128))
    ns = 1
    w2 = 2 * BRANCH_WIDTH
    lb = functools.partial(_layer_block, layer=layer)
    tab = lb((S5_T, S5_STATES))
    rowv = lb((1, S5_STATES))
    wb = lb((S5_SLABS, S5_SLAB_IN, S5_SLAB_ST))
    cb = lb((S5_SLABS, S5_SLAB_ST, S5_SLAB_IN))
    vec = lb((1, BRANCH_WIDTH))
    out = pl.pallas_call(
        functools.partial(_s5_kernel, rb=rb, ns=ns),
        grid=(bsz // ns, lp // rb),
        in_specs=[pl.BlockSpec((ns, rb, w2), lambda b, r: (b, r, 0)),
                  wb, wb, tab, tab, tab, tab, rowv, rowv, rowv, rowv, cb, cb, vec,
                  lb((BRANCH_WIDTH, BRANCH_WIDTH)), vec],
        out_specs=pl.BlockSpec((ns, rb, BRANCH_WIDTH), lambda b, r: (b, r, 0)),
        out_shape=jax.ShapeDtypeStruct((bsz, lp, BRANCH_WIDTH), BF16),
        scratch_shapes=[pltpu.VMEM((ns, rb, S5_STATES), BF16), pltpu.VMEM((ns, rb, S5_STATES), BF16),
                        pltpu.VMEM((ns, rb, S5_STATES), BF16), pltpu.VMEM((ns, rb, S5_STATES), BF16),
                        pltpu.VMEM((ns, 1, S5_STATES), F32), pltpu.VMEM((ns, 1, S5_STATES), F32)],
        compiler_params=_params("parallel", "arbitrary"),
        name="s5_branch",
    )(uz.reshape(bsz, lp, w2), prm["wb_re"], prm["wb_im"], prm["pn_re"], prm["pn_im"], prm["pp_re"], prm["pp_im"],
      prm["ab_re"], prm["ab_im"], prm["pl_re"], prm["pl_im"], prm["c_re"], prm["c_im"],
      prm["d"], prm["w_glu"], prm["b_glu"])
    return out.reshape(bsz * lp, BRANCH_WIDTH)


def _s5_prepare(a_re, a_im, log_step, b_re, b_im, c_re, c_im, d, w_glu, b_glu):
    nl = a_re.shape[0]
    lam_re = jnp.minimum(a_re, -1e-4)
    lam_im = a_im
    step = jnp.exp(log_step)[..., None]
    dec = lam_re * step
    ang = lam_im * step
    mag = jnp.exp(dec)
    abar_re, abar_im = mag * jnp.cos(ang), mag * jnp.sin(ang)
    den = lam_re * lam_re + lam_im * lam_im
    nr, ni = abar_re - 1.0, abar_im
    coef_re = (nr * lam_re + ni * lam_im) / den
    coef_im = (ni * lam_re - nr * lam_im) / den
    bbar_re = coef_re[..., None] * b_re - coef_im[..., None] * b_im
    bbar_im = coef_re[..., None] * b_im + coef_im[..., None] * b_re

    gps = S5_SLAB_IN // S5_GROUP
    eye = jnp.eye(gps, dtype=F32)

    def in_slabs(bb):
        w = jnp.einsum('lkgpc,gh->lkgchp', bb.reshape(nl, S5_SLABS, gps, S5_STATE, S5_GROUP), eye)
        return w.reshape(nl, S5_SLABS, S5_SLAB_IN, S5_SLAB_ST).astype(BF16)

    def out_slabs(cc):
        w = jnp.einsum('lkgcp,gh->lkgphc', cc.reshape(nl, S5_SLABS, gps, S5_GROUP, S5_STATE), eye)
        return w.reshape(nl, S5_SLABS, S5_SLAB_ST, S5_SLAB_IN).astype(BF16)

    idx = jnp.arange(S5_T, dtype=F32)[None, :, None]
    dec = dec.reshape(nl, 1, S5_STATES)
    ang = ang.reshape(nl, 1, S5_STATES)
    cs, sn = jnp.cos(idx * ang), jnp.sin(idx * ang)
    grow, shrink = jnp.exp(idx * dec), jnp.exp(-idx * dec)
    pp_re, pp_im = grow * cs, grow * sn
    return dict(wb_re=in_slabs(bbar_re), wb_im=in_slabs(bbar_im),
                pn_re=(shrink * cs).astype(BF16), pn_im=(-shrink * sn).astype(BF16),
                pp_re=pp_re.astype(BF16), pp_im=pp_im.astype(BF16),
                ab_re=abar_re.reshape(nl, 1, S5_STATES), ab_im=abar_im.reshape(nl, 1, S5_STATES),
                pl_re=pp_re[:, S5_T - 1:], pl_im=pp_im[:, S5_T - 1:],
                c_re=out_slabs(c_re), c_im=out_slabs(c_im), d=d.reshape(nl, 1, -1),
                w_glu=w_glu.astype(BF16), b_glu=b_glu.reshape(nl, 1, -1))


def _conv_begin(buf, carry, x_ref, width, first):
    rb = x_ref.shape[0]

    @pl.when(first)
    def _():
        carry[...] = jnp.zeros_like(carry)

    buf[pl.ds(0, CONV_CARRY), :] = carry[...]
    buf[pl.ds(CONV_CARRY, rb), :] = x_ref[:, :width]
    carry[...] = x_ref[rb - CONV_CARRY:rb, :width]


def _causal_conv(buf, c, x, shift_ref, w, q):
    taps = _dot(shift_ref[...], buf[pl.ds(c * q, q + CONV_CARRY), :])
    acc = x.astype(F32) * w[CONV_K - 1:CONV_K, :]
    for j in range(CONV_K - 1):
        acc = acc + taps[j * q:(j + 1) * q, :] * w[j:j + 1, :]
    return acc


def _tri_masks(q):
    row = lax.broadcasted_iota(jnp.int32, (q, q), 0)
    col = lax.broadcasted_iota(jnp.int32, (q, q), 1)
    return row >= col, row > col, row == col


def _ssd_kernel(xz_ref, sm_ref, shift_ref, cw_ref, cb_ref, dtb_ref, nega_ref, e64_ref, e128_ref,
                d_ref, ng_ref, out_ref, buf, carry, state, *, q, nc):
    first = pl.program_id(1) == 0

    @pl.when(first)
    def _():
        state[...] = jnp.zeros_like(state)

    _conv_begin(buf, carry, xz_ref, SSD_CONV_WIDTH, first)

    causal, _, _ = _tri_masks(q)
    tril = jnp.where(causal, 1.0, 0.0).astype(BF16)
    low_half = lax.broadcasted_iota(jnp.int32, (q, LANES), 1) < SSD_HEAD_DIM

    pre = []
    for c in range(nc):
        rows = slice(c * q, (c + 1) * q)
        xbc = _silu(_causal_conv(buf, c, xz_ref[rows, :SSD_CONV_WIDTH], shift_ref, cw_ref[...], q) + cb_ref[...])
        xs = xbc[:, :BRANCH_WIDTH]
        bmat = xbc[:, BRANCH_WIDTH:BRANCH_WIDTH + SSD_BC]
        cmat = xbc[:, BRANCH_WIDTH + SSD_BC:]
        dt = _softplus(sm_ref[rows, :] + dtb_ref[...])
        da = dt * nega_ref[...]
        acum = _cumsum_rows(tril, da)
        acum_t = acum.T
        a3 = _split3(acum)
        acum64 = _dot(a3, e64_ref[...])
        acum128 = _dot(a3, e128_ref[...])
        dt64 = _dot(_split3(dt), e64_ref[...])
        alast = acum64[q - 1:q, :]
        xd = xs * dt64
        bgs = [bmat[:, g * SSD_STATE:(g + 1) * SSD_STATE] for g in range(SSD_GROUPS)]
        cgs = [cmat[:, g * SSD_STATE:(g + 1) * SSD_STATE].astype(BF16) for g in range(SSD_GROUPS)]
        scores = [_dot_nt(cgs[g], bgs[g].astype(BF16)) for g in range(SSD_GROUPS)]
        ms, rhs = [], []
        for h in range(SSD_HEADS):
            lmat = jnp.where(causal, jnp.exp(acum128[:, h * LANES:(h + 1) * LANES] - acum_t[h:h + 1, :]), 0.0)
            ms.append((scores[h // SSD_HPG] * lmat).astype(BF16))
            pair = xd[:, (h // 2) * LANES:(h // 2 + 1) * LANES]
            keep = low_half if h % 2 == 0 else jnp.logical_not(low_half)
            rhs.append(jnp.where(keep, pair, 0.0).astype(BF16))
        ypairs = [_dot(ms[2 * p], rhs[2 * p]) + _dot(ms[2 * p + 1], rhs[2 * p + 1])
                  for p in range(SSD_HEADS // 2)]
        pre.append(dict(
            y=jnp.concatenate(ypairs, axis=-1) + xs * d_ref[...], cgs=cgs,
            bts=[bgs[g].T.astype(BF16) for g in range(SSD_GROUPS)],
            e_in=jnp.exp(acum64),
            xdw=(xd * jnp.exp(alast - acum64)).astype(BF16),
            e_all=jnp.exp(alast)))

    sts = [state[g] for g in range(SSD_GROUPS)]
    for c in range(nc):
        rows = slice(c * q, (c + 1) * q)
        pc = pre[c]
        y_off = jnp.concatenate([_dot(pc["cgs"][g], sts[g].astype(BF16)) for g in range(SSD_GROUPS)], axis=-1)
        for g in range(SSD_GROUPS):
            gs = slice(g * SSD_GW, (g + 1) * SSD_GW)
            sts[g] = sts[g] * pc["e_all"][:, gs] + _dot(pc["bts"][g], pc["xdw"][:, gs])
        y = (pc["y"] + y_off * pc["e_in"]) * _silu(xz_ref[rows, SSD_CONV_WIDTH:].astype(F32))
        y = y * lax.rsqrt(jnp.mean(y * y, axis=-1, keepdims=True) + LN_EPS) * ng_ref[...]
        out_ref[rows, :] = y.astype(out_ref.dtype)
    for g in range(SSD_GROUPS):
        state[g] = sts[g]


def _ssd_branch(xz, small, prm, layer, bsz, lp):
    q = CHUNK
    rb = _pick(lp, (3 * q, 2 * q, q))
    nblk = lp // rb
    wx = SSD_CONV_WIDTH + BRANCH_WIDTH
    rowmap = lambda b, r: (b * nblk + r, 0)
    lb = functools.partial(_layer_block, layer=layer)
    shift = _shift_matrix(q)
    e64 = _expand_matrix(SM_SSD_DT, SSD_HEADS, SSD_HEAD_DIM, 3)
    e128 = _expand_matrix(SM_SSD_DT, SSD_HEADS, LANES, 3)
    return pl.pallas_call(
        functools.partial(_ssd_kernel, q=q, nc=rb // q),
        grid=(bsz, nblk),
        in_specs=[pl.BlockSpec((rb, wx), rowmap), pl.BlockSpec((rb, LANES), rowmap),
                  _full(shift.shape), lb((CONV_K, SSD_CONV_WIDTH)), lb((1, SSD_CONV_WIDTH)),
                  lb((1, LANES)), lb((1, LANES)), _full(e64.shape), _full(e128.shape),
                  lb((1, BRANCH_WIDTH)), lb((1, BRANCH_WIDTH))],
        out_specs=pl.BlockSpec((rb, BRANCH_WIDTH), rowmap),
        out_shape=jax.ShapeDtypeStruct((bsz * lp, BRANCH_WIDTH), BF16),
        scratch_shapes=[pltpu.VMEM((rb + CONV_CARRY, SSD_CONV_WIDTH), BF16),
                        pltpu.VMEM((CONV_CARRY, SSD_CONV_WIDTH), BF16),
                        pltpu.VMEM((SSD_GROUPS, SSD_STATE, SSD_GW), F32)],
        compiler_params=_params("parallel", "arbitrary"),
        name="ssd_branch",
    )(xz, small, shift, prm["conv_w"], prm["conv_b"], prm["dt_bias"], prm["neg_a"], e64, e128,
      prm["d"], prm["norm_g"])


def _lane_vectors(values, offset):
    nl, n = values.shape
    return jnp.pad(values.astype(F32), ((0, 0), (offset, LANES - offset - n))).reshape(nl, 1, LANES)


def _ssd_prepare(conv_w, conv_b, dt_bias, a_log, d, norm_g):
    nl = conv_w.shape[0]
    return dict(conv_w=conv_w, conv_b=conv_b.reshape(nl, 1, -1),
                dt_bias=_lane_vectors(dt_bias, SM_SSD_DT),
                neg_a=_lane_vectors(-jnp.exp(a_log), SM_SSD_DT),
                d=jnp.repeat(d, SSD_HEAD_DIM, axis=1).reshape(nl, 1, -1), norm_g=norm_g.reshape(nl, 1, -1))


def _head_sums(x, ones_ref):
    outs = []
    for s in range(x.shape[1] // (2 * LANES)):
        xs = x[:, s * 2 * LANES:(s + 1) * 2 * LANES]
        outs.append(_dot(xs.astype(BF16), ones_ref[...]))
    return jnp.concatenate(outs, axis=-1)


def _gdn_kernel(qkvz_ref, sm_ref, shift_ref, cw_ref, dtb_ref, nega_ref, eg_ref, eb_ref, ones_ref, ng_ref,
                out_ref, buf, carry, state, *, q, nc, ns, lr):
    first = pl.program_id(1) == 0

    @pl.when(first)
    def _():
        state[...] = jnp.zeros_like(state)

    for s in range(ns):
        _conv_begin(buf.at[s], carry.at[s], qkvz_ref.at[s], GDN_QKV, first)
    cw = cw_ref[...]

    causal, strict, _ = _tri_masks(q)
    tril = jnp.where(causal, 1.0, 0.0).astype(BF16)

    pre = {}
    for s, c in [(s, c) for s in range(ns) for c in range(nc)]:
        rows = slice(c * q, (c + 1) * q)
        qkv = _silu(_causal_conv(buf.at[s], c, qkvz_ref[s, rows, :GDN_QKV], shift_ref, cw, q))
        qa = qkv[:, :BRANCH_WIDTH]
        ka = qkv[:, BRANCH_WIDTH:2 * BRANCH_WIDTH]
        va = qkv[:, 2 * BRANCH_WIDTH:]
        qa = qa * (lax.rsqrt(_head_sums(qa * qa, ones_ref) + 1e-6) * (GDN_HEAD_DIM ** -0.5))
        ka = ka * lax.rsqrt(_head_sums(ka * ka, ones_ref) + 1e-6)
        sm = sm_ref[s, rows, :]
        glog = nega_ref[...] * _softplus(sm + dtb_ref[...])
        gcum = _cumsum_rows(tril, glog)
        g128 = _dot(_split3(gcum), eg_ref[...])
        beta = _dot(_split2(_sigmoid(sm)), eb_ref[...])
        rid = lax.broadcasted_iota(jnp.int32, beta.shape, 0) + (pl.program_id(1) * (nc * q) + c * q)
        beta = jnp.where(rid < lr, beta, 0.0)
        glast = g128[q - 1:q, :]
        e_in = jnp.exp(g128)
        pre[s, c] = dict(qa=qa, ka=ka, beta=beta, g128=g128, gcum_t=gcum.T, e_all=jnp.exp(glast),
                         vbeta=va * beta, kbe=ka * (beta * e_in), qg=qa * e_in, kd=ka * jnp.exp(glast - g128))

    dk = GDN_HEAD_DIM
    npair = GDN_HEADS // 2
    units = [(s, c, p) for s in range(ns) for c in range(nc) for p in range(npair)]
    ps = [slice(p * 2 * dk, (p + 1) * 2 * dk) for p in range(npair)]
    left = lax.broadcasted_iota(jnp.int32, (q, 2 * dk), 1) < dk

    def blockdiag(x):
        return jnp.concatenate([jnp.where(left, x, 0.0), jnp.where(left, 0.0, x)], axis=0).astype(BF16)

    causal2 = jnp.concatenate([causal, causal], axis=1)
    strict2 = jnp.concatenate([strict, strict], axis=1)
    gammas, pws, tinvs, attns = {}, {}, {}, {}
    for s, c, p in units:
        la = SM_GDN_A + 2 * p
        gt = pre[s, c]["gcum_t"]
        grow = jnp.concatenate([gt[la:la + 1, :], gt[la + 1:la + 2, :]], axis=1)
        gammas[s, c, p] = jnp.where(causal2, jnp.exp(pre[s, c]["g128"][:, ps[p]] - grow), 0.0)
    for s, c, p in units:
        kp = pre[s, c]["ka"][:, ps[p]]
        lhs = jnp.concatenate([kp.astype(BF16), pre[s, c]["qa"][:, ps[p]].astype(BF16)], axis=0)
        kq = _dot_nt(lhs, blockdiag(kp))
        pws[s, c, p] = jnp.where(strict2, -(kq[:q] * gammas[s, c, p] * pre[s, c]["beta"][:, ps[p]]), 0.0)
        attns[s, c, p] = (kq[q:] * gammas[s, c, p]).astype(BF16)
    bfull = dict(pws)
    b = GDN_INV_BASE
    per_head = q // b

    def lane_blocks(bs):
        return lax.broadcasted_iota(jnp.int32, (bs, 2 * q), 1) // bs

    blk = lane_blocks(b)
    eyes = jnp.where(lax.broadcasted_iota(jnp.int32, (b, 2 * q), 1) % b
                     == lax.broadcasted_iota(jnp.int32, (b, 2 * q), 0), 1.0, 0.0)

    def blockdiag_b(x):
        return jnp.concatenate([jnp.where(blk == i, x, 0.0) for i in range(2 * q // b)], axis=0).astype(BF16)

    for u in units:
        diag = bfull[u][:b]
        for j in range(1, per_head):
            diag = jnp.where(blk % per_head == j, bfull[u][j * b:(j + 1) * b], diag)
        tinvs[u] = eyes + diag
        pws[u] = _dot(diag.astype(BF16), blockdiag_b(diag))
    steps = int(math.log2(b)) - 1
    for i in range(steps):
        last = i == steps - 1
        for u in units:
            pwb = pws[u].astype(BF16)
            lhs = tinvs[u].astype(BF16) if last else jnp.concatenate([tinvs[u].astype(BF16), pwb], axis=0)
            both = _dot(lhs, blockdiag_b(pws[u]))
            tinvs[u] = tinvs[u] + both[:b]
            if not last:
                pws[u] = both[b:]
    while b < q:
        nblk = 2 * q // b
        blk = lane_blocks(b)
        even = blk % 2 == 0
        zeros = jnp.zeros((b, 2 * q), F32)
        t2b = {}
        for u in units:
            rows = [zeros if i % 2 == 0 else
                    jnp.where(blk == i - 1, bfull[u][(i % (q // b)) * b:(i % (q // b) + 1) * b], 0.0)
                    for i in range(nblk)]
            t2b[u] = _dot(jnp.where(even, 0.0, tinvs[u]).astype(BF16), jnp.concatenate(rows, axis=0).astype(BF16))
        for u in units:
            td = tinvs[u]
            rows = [jnp.where(blk == i, td, 0.0) if i % 2 == 0 else zeros for i in range(nblk)]
            t21 = _dot(t2b[u].astype(BF16), jnp.concatenate(rows, axis=0).astype(BF16))
            tinvs[u] = jnp.concatenate([jnp.where(even, td, 0.0), jnp.where(even, t21, td)], axis=0)
        b *= 2
    us, ws = {}, {}
    for s, c, p in units:
        uw = []
        for i in range(2):
            hcol = slice((2 * p + i) * dk, (2 * p + i + 1) * dk)
            rhs = jnp.concatenate([pre[s, c]["vbeta"][:, hcol], pre[s, c]["kbe"][:, hcol]], axis=-1)
            uw.append(_dot(tinvs[s, c, p][:, i * dk:(i + 1) * dk].astype(BF16), rhs.astype(BF16)))
        us[s, c, p] = jnp.concatenate([uw[0][:, :dk], uw[1][:, :dk]], axis=-1)
        ws[s, c, p] = jnp.concatenate([uw[0][:, dk:], uw[1][:, dk:]], axis=-1)

    sts = {(s, h): state[s, h] for s in range(ns) for h in range(GDN_HEADS)}
    zero = jnp.zeros((dk, dk), F32)
    for c in range(nc):
        rows = slice(c * q, (c + 1) * q)
        wq = {}
        for s in range(ns):
            for p in range(npair):
                sbd = jnp.concatenate([jnp.concatenate([sts[s, 2 * p], zero], axis=1),
                                       jnp.concatenate([zero, sts[s, 2 * p + 1]], axis=1)], axis=0)
                lhs = jnp.concatenate([ws[s, c, p].astype(BF16), pre[s, c]["qg"][:, ps[p]].astype(BF16)], axis=0)
                wq[s, p] = _dot(lhs, sbd.astype(BF16))
        for s in range(ns):
            outs = []
            for p in range(npair):
                vnew = us[s, c, p] - wq[s, p][:q]
                outs.append(wq[s, p][q:] + _dot(attns[s, c, p], blockdiag(vnew)))
                vb = vnew.astype(BF16)
                for i in range(2):
                    h = 2 * p + i
                    hcol = slice(h * dk, (h + 1) * dk)
                    kdt = pre[s, c]["kd"][:, hcol].T.astype(BF16)
                    sts[s, h] = sts[s, h] * pre[s, c]["e_all"][:, hcol] + _dot(kdt, vb[:, i * dk:(i + 1) * dk])
            o = jnp.concatenate(outs, axis=-1)
            o = o * lax.rsqrt(_head_sums(o * o, ones_ref) * (1.0 / dk) + LN_EPS)
            gate = _silu(qkvz_ref[s, rows, GDN_QKV:].astype(F32))
            o = o * jnp.concatenate([ng_ref[...]] * GDN_HEADS, axis=-1) * gate
            out_ref[s, rows, :] = o.astype(out_ref.dtype)
    for s in range(ns):
        for h in range(GDN_HEADS):
            state[s, h] = sts[s, h]


def _gdn_branch(qkvz, small, prm, layer, bsz, lp, lr):
    q = CHUNK
    assert q == GDN_HEAD_DIM
    rb = _pick(lp, (3 * q, 2 * q, q))
    ns = 1
    wx = GDN_QKV + BRANCH_WIDTH
    rowmap = lambda b, r: (b, r, 0)
    lb = functools.partial(_layer_block, layer=layer)
    shift = _shift_matrix(q)
    e_g = _expand_matrix(SM_GDN_A, GDN_HEADS, GDN_HEAD_DIM, 3)
    e_b = _expand_matrix(SM_GDN_B, GDN_HEADS, GDN_HEAD_DIM, 2)
    ones = _block_ones(2 * LANES, GDN_HEAD_DIM, 1)
    out = pl.pallas_call(
        functools.partial(_gdn_kernel, q=q, nc=rb // q, ns=ns, lr=lr),
        grid=(bsz // ns, lp // rb),
        in_specs=[pl.BlockSpec((ns, rb, wx), rowmap), pl.BlockSpec((ns, rb, LANES), rowmap),
                  _full(shift.shape), lb((CONV_K, GDN_QKV)), lb((1, LANES)), lb((1, LANES)),
                  _full(e_g.shape), _full(e_b.shape), _full(ones.shape), lb((1, GDN_HEAD_DIM))],
        out_specs=pl.BlockSpec((ns, rb, BRANCH_WIDTH), rowmap),
        out_shape=jax.ShapeDtypeStruct((bsz, lp, BRANCH_WIDTH), BF16),
        scratch_shapes=[pltpu.VMEM((ns, rb + CONV_CARRY, GDN_QKV), BF16),
                        pltpu.VMEM((ns, CONV_CARRY, GDN_QKV), BF16),
                        pltpu.VMEM((ns, GDN_HEADS, GDN_HEAD_DIM, GDN_HEAD_DIM), F32)],
        compiler_params=_params("parallel", "arbitrary"),
        name="gdn_branch",
    )(qkvz.reshape(bsz, lp, wx), small.reshape(bsz, lp, LANES), shift, prm["conv_w"], prm["dt_bias"],
      prm["neg_a"], e_g, e_b, ones, prm["norm_g"])
    return out.reshape(bsz * lp, BRANCH_WIDTH)


def _gdn_prepare(conv_w, dt_bias, a_log, norm_g):
    nl = conv_w.shape[0]
    return dict(conv_w=conv_w, dt_bias=_lane_vectors(dt_bias, SM_GDN_A),
                neg_a=_lane_vectors(-jnp.exp(a_log), SM_GDN_A), norm_g=norm_g.reshape(nl, 1, -1))


def _merge_kernel(h_ref, ya_ref, yb_ref, yc_ref, wg_ref, bg_ref, wbr_ref, wo_ref, g_ref, b_ref,
                  o_ref, ob_ref, *, alpha):
    h = h_ref[...]
    logits = _dot(h.astype(BF16), wg_ref[...]) + bg_ref[...]
    merged = None
    for n, y_ref in enumerate((ya_ref, yb_ref, yc_ref)):
        gate = _sigmoid(logits[:, n * D_MODEL:(n + 1) * D_MODEL])
        term = gate * _dot(y_ref[...], wbr_ref[n])
        merged = term if merged is None else merged + term
    y = alpha * h + _dot(merged.astype(BF16), wo_ref[...])
    y = _ln_math(y, g_ref[...], b_ref[...])
    o_ref[...] = y
    ob_ref[...] = y.astype(BF16)


def _merge(h, ya, yb, yc, prm, layer, alpha):
    m, d = h.shape
    tm = _pick(m, (512, 256, 128))
    rows = lambda w: pl.BlockSpec((tm, w), lambda i: (i, 0))
    lb = functools.partial(_layer_block, layer=layer)
    return pl.pallas_call(
        functools.partial(_merge_kernel, alpha=alpha),
        grid=(m // tm,),
        in_specs=[rows(d), rows(BRANCH_WIDTH), rows(BRANCH_WIDTH), rows(BRANCH_WIDTH),
                  lb((d, N_BRANCH * d)), lb((1, N_BRANCH * d)),
                  lb((N_BRANCH, BRANCH_WIDTH, d)), lb((d, d)), lb((1, d)), lb((1, d))],
        out_specs=[rows(d), rows(d)],
        out_shape=[jax.ShapeDtypeStruct((m, d), F32), jax.ShapeDtypeStruct((m, d), BF16)],
        compiler_params=_params("parallel"),
        name="merge",
    )(h, ya, yb, yc, prm["w_gate"], prm["b_gate"], prm["w_branch"], prm["w_out"], prm["ln_g"], prm["ln_b"])


def _split_in_proj(w):
    widths = (BRANCH_WIDTH, BRANCH_WIDTH, SSD_CONV_WIDTH, SSD_HEADS, BRANCH_WIDTH,
              GDN_QKV, GDN_HEADS, GDN_HEADS, BRANCH_WIDTH, N_BRANCH * D_MODEL)
    offs = [0]
    for wd in widths:
        offs.append(offs[-1] + wd)
    seg = [w[:, :, offs[i]:offs[i + 1]] for i in range(len(widths))]
    s5_u, s5_z, ssd_xbc, ssd_dt, ssd_z, gdn_qkv, gdn_a, gdn_b, gdn_z, gate = seg
    pad = jnp.zeros(w.shape[:2] + (LANES - SSD_HEADS - 2 * GDN_HEADS,), w.dtype)
    return dict(
        s5=jnp.concatenate([s5_u, s5_z], axis=2).astype(BF16),
        ssd=jnp.concatenate([ssd_xbc, ssd_z], axis=2).astype(BF16),
        gdn=jnp.concatenate([gdn_qkv, gdn_z], axis=2).astype(BF16),
        small=jnp.concatenate([ssd_dt, gdn_a, gdn_b, pad], axis=2).astype(BF16),
        gate=gate.astype(BF16))


def kernel(x, meta, ln_in_g, ln_in_b, w_in, s5_a_re, s5_a_im, s5_log_step, s5_b_re, s5_b_im, s5_c_re, s5_c_im, s5_d, s5_w_glu, s5_b_glu, ssd_conv_w, ssd_conv_b, ssd_dt_bias, ssd_a_log, ssd_d, ssd_norm_g, gdn_conv_w, gdn_dt_bias, gdn_a_log, gdn_norm_g, w_branch, b_gate, w_out, ln_g, ln_b):
    bsz, seq, d = x.shape
    depth = w_in.shape[0]
    alpha = (2 * depth) ** 0.25
    lr = N_META + seq
    lp = -(-lr // CHUNK) * CHUNK
    h0 = jnp.concatenate([jnp.broadcast_to(meta[None].astype(x.dtype), (bsz, N_META, d)), x,
                          jnp.zeros((bsz, lp - lr, d), x.dtype)], axis=1).reshape(bsz * lp, d)
    h, hb = _layer_norm_in(h0, ln_in_g, ln_in_b)

    wi = _split_in_proj(w_in)
    s5p = _s5_prepare(s5_a_re, s5_a_im, s5_log_step, s5_b_re, s5_b_im, s5_c_re, s5_c_im, s5_d, s5_w_glu, s5_b_glu)
    ssdp = _ssd_prepare(ssd_conv_w, ssd_conv_b, ssd_dt_bias, ssd_a_log, ssd_d, ssd_norm_g)
    gdnp = _gdn_prepare(gdn_conv_w, gdn_dt_bias, gdn_a_log, gdn_norm_g)
    mrg = dict(w_gate=wi["gate"], b_gate=b_gate.reshape(depth, 1, -1), w_branch=w_branch.astype(BF16),
               w_out=w_out.astype(BF16), ln_g=ln_g.reshape(depth, 1, -1), ln_b=ln_b.reshape(depth, 1, -1))

    for layer in range(depth):
        p_s5, p_ssd, p_gdn, small = _in_proj(hb, [wi["s5"], wi["ssd"], wi["gdn"], wi["small"]], layer,
                                             [BF16, BF16, BF16, F32])
        y_a = _s5_branch(p_s5, s5p, layer, bsz, lp)
        y_b = _ssd_branch(p_ssd, small, ssdp, layer, bsz, lp)
        y_c = _gdn_branch(p_gdn, small, gdnp, layer, bsz, lp, lr)
        h, hb = _merge(h, y_a, y_b, y_c, mrg, layer, alpha)

    return h.reshape(bsz, lp, d)[:, N_META:lr]
```

```python
import functools
import math

import numpy as np
import jax
import jax.numpy as jnp
from jax import lax
from jax.experimental import pallas as pl
from jax.experimental.pallas import tpu as pltpu

D_MODEL = 1024
N_META = 16
CONV_K = 4
N_BRANCH = 3
BRANCH_WIDTH = 768

S5_GROUP = 16
S5_GROUPS = BRANCH_WIDTH // S5_GROUP
S5_STATE = 64
S5_STATES = S5_GROUPS * S5_STATE
S5_SLAB_IN = 256
S5_SLAB_ST = 1024
S5_SLABS = BRANCH_WIDTH // S5_SLAB_IN
S5_T = 128
S5_CARRY_ROWS = 16

SSD_HEAD_DIM = 64
SSD_HEADS = 12
SSD_GROUPS = 2
SSD_HPG = SSD_HEADS // SSD_GROUPS
SSD_STATE = 128
SSD_BC = SSD_GROUPS * SSD_STATE
SSD_CONV_WIDTH = BRANCH_WIDTH + 2 * SSD_BC
SSD_GW = SSD_HPG * SSD_HEAD_DIM

GDN_HEAD_DIM = 128
GDN_HEADS = 6
GDN_QKV = 3 * BRANCH_WIDTH
GDN_INV_BASE = 64

CHUNK = 128
LANES = 128
CONV_CARRY = 16
LN_EPS = 1e-5

SM_SSD_DT = 0
SM_GDN_A = SSD_HEADS
SM_GDN_B = SSD_HEADS + GDN_HEADS

VMEM_LIMIT = 56 * 1024 * 1024

F32 = jnp.float32
BF16 = jnp.bfloat16


def _dot(a, b):
    return jnp.dot(a, b, preferred_element_type=F32)


def _cumsum_rows(tril, x):
    parts = _dot(tril, _split3(x))
    return parts[:, :LANES] + parts[:, LANES:2 * LANES] + parts[:, 2 * LANES:]


def _dot_nt(a, b):
    return lax.dot_general(a, b, (((1,), (1,)), ((), ())), preferred_element_type=F32)


def _sigmoid(x):
    return 1.0 / (1.0 + jnp.exp(-x))


def _silu(x):
    return x * _sigmoid(x)


def _softplus(x):
    return jnp.maximum(x, 0.0) + jnp.log1p(jnp.exp(-jnp.abs(x)))


def _gelu_tanh(x):
    c = math.sqrt(2.0 / math.pi)
    return 0.5 * x * (1.0 + jnp.tanh(c * (x + 0.044715 * (x * x * x))))


def _bf16_floor(x):
    bits = lax.bitcast_convert_type(x, jnp.uint32) & jnp.uint32(0xFFFF0000)
    return lax.bitcast_convert_type(bits, F32)


def _split3(x):
    hi = _bf16_floor(x)
    r1 = x - hi
    mid = _bf16_floor(r1)
    return jnp.concatenate([hi.astype(BF16), mid.astype(BF16), (r1 - mid).astype(BF16)], axis=-1)


def _split2(x):
    hi = _bf16_floor(x)
    return jnp.concatenate([hi.astype(BF16), (x - hi).astype(BF16)], axis=-1)


def _pick(n, candidates):
    for c in candidates:
        if n % c == 0:
            return c
    raise ValueError(f"no tile for {n} in {candidates}")


def _params(*semantics):
    return pltpu.CompilerParams(dimension_semantics=semantics, vmem_limit_bytes=VMEM_LIMIT)


def _full(shape):
    n = len(shape)
    return pl.BlockSpec(shape, lambda *_: (0,) * n)


def _layer_block(shape, layer):
    n = len(shape)
    return pl.BlockSpec((None,) + tuple(shape), lambda *_: (layer,) + (0,) * n)


def _expand_matrix(first_lane, heads, width, pieces):
    e = np.zeros((LANES, heads * width), np.float32)
    for h in range(heads):
        e[first_lane + h, h * width:(h + 1) * width] = 1.0
    return jnp.asarray(np.tile(e, (pieces, 1)), BF16)


def _shift_matrix(q):
    s = np.zeros(((CONV_K - 1) * q, q + CONV_CARRY), np.float32)
    for j in range(CONV_K - 1):
        for t in range(q):
            s[j * q + t, CONV_CARRY + t - (CONV_K - 1) + j] = 1.0
    return jnp.asarray(s, BF16)


def _block_ones(width, block, pieces):
    o = np.kron(np.eye(width // block, dtype=np.float32), np.ones((block, block), np.float32))
    return jnp.asarray(np.tile(o, (pieces, 1)), BF16)


def _ln_math(x, g, b):
    mu = jnp.mean(x, axis=-1, keepdims=True)
    xc = x - mu
    var = jnp.mean(xc * xc, axis=-1, keepdims=True)
    return xc * lax.rsqrt(var + LN_EPS) * g + b


def _ln_kernel(x_ref, g_ref, b_ref, o_ref, ob_ref):
    y = _ln_math(x_ref[...], g_ref[...], b_ref[...])
    o_ref[...] = y
    ob_ref[...] = y.astype(BF16)


def _layer_norm_in(x2d, g, b):
    m, d = x2d.shape
    tm = _pick(m, (512, 256, 128))
    return pl.pallas_call(
        _ln_kernel,
        grid=(m // tm,),
        in_specs=[pl.BlockSpec((tm, d), lambda i: (i, 0)), _full((1, d)), _full((1, d))],
        out_specs=[pl.BlockSpec((tm, d), lambda i: (i, 0)), pl.BlockSpec((tm, d), lambda i: (i, 0))],
        out_shape=[jax.ShapeDtypeStruct((m, d), F32), jax.ShapeDtypeStruct((m, d), BF16)],
        compiler_params=_params("parallel"),
        name="ln_in",
    )(x2d, g.reshape(1, d), b.reshape(1, d))


def _in_proj_kernel(x_ref, *refs):
    n = len(refs) // 2
    x = x_ref[...]
    for w_ref, o_ref in zip(refs[:n], refs[n:]):
        o_ref[...] = _dot(x, w_ref[...]).astype(o_ref.dtype)


def _in_proj(x, weights, layer, out_dtypes):
    m, k = x.shape
    tm = _pick(m, (768, 512, 256, 128))
    widths = [w.shape[-1] for w in weights]
    rows = lambda n: pl.BlockSpec((tm, n), lambda i: (i, 0))
    return pl.pallas_call(
        _in_proj_kernel,
        grid=(m // tm,),
        in_specs=[rows(k)] + [_layer_block((k, n), layer) for n in widths],
        out_specs=[rows(n) for n in widths],
        out_shape=[jax.ShapeDtypeStruct((m, n), dt) for n, dt in zip(widths, out_dtypes)],
        compiler_params=_params("parallel"),
        name="in_proj",
    )(x, *weights)


def _carry_rows(c):
    hi = _bf16_floor(c)
    r1 = c - hi
    mid = _bf16_floor(r1)
    lo = r1 - mid
    rid = lax.broadcasted_iota(jnp.int32, (S5_CARRY_ROWS, c.shape[1]), 0)
    ext = jnp.where(rid == 0, hi, jnp.where(rid == 1, mid, jnp.where(rid == 2, lo, 0.0)))
    return ext.astype(BF16)


def _s5_kernel(uz_ref, wbre_ref, wbim_ref, pnr_ref, pni_ref, ppr_ref, ppi_ref, abr_ref, abi_ref,
               plr_ref, pli_ref, cre_ref, cim_ref, d_ref, wglu_ref, bglu_ref, out_ref,
               bure, buim, sre, sim, st_re, st_im, *, rb, ns):
    @pl.when(pl.program_id(1) == 0)
    def _():
        st_re[...] = jnp.zeros_like(st_re)
        st_im[...] = jnp.zeros_like(st_im)

    streams = range(ns)
    slabs = [slice(k * S5_SLAB_ST, (k + 1) * S5_SLAB_ST) for k in range(S5_SLABS)]
    us = [uz_ref[s, :, :BRANCH_WIDTH] for s in streams]
    for s in streams:
        for k in range(S5_SLABS):
            uk = us[s][:, k * S5_SLAB_IN:(k + 1) * S5_SLAB_IN]
            bure[s, :, slabs[k]] = _dot(uk, wbre_ref[k]).astype(BF16)
            buim[s, :, slabs[k]] = _dot(uk, wbim_ref[k]).astype(BF16)

    t = S5_T
    row = lax.broadcasted_iota(jnp.int32, (t, t + S5_CARRY_ROWS), 0)
    col = lax.broadcasted_iota(jnp.int32, (t, t + S5_CARRY_ROWS), 1)
    ones_cols = jnp.where(col >= t, 1.0, 0.0) * jnp.where(col < t + 3, 1.0, 0.0)
    tril = (jnp.where(row >= col, 1.0, 0.0) + ones_cols).astype(BF16)

    for c in range(rb // t):
        rows = slice(c * t, (c + 1) * t)
        for s in streams:
            for sl in slabs:
                br = bure[s, rows, sl]
                bi = buim[s, rows, sl]
                pnr = pnr_ref[:, sl]
                pni = pni_ref[:, sl]
                s_in_r = st_re[s, :, sl]
                s_in_i = st_im[s, :, sl]
                abr = abr_ref[:, sl]
                abi = abi_ref[:, sl]
                xr = jnp.concatenate([br * pnr - bi * pni, _carry_rows(abr * s_in_r - abi * s_in_i)], axis=0)
                xi = jnp.concatenate([br * pni + bi * pnr, _carry_rows(abr * s_in_i + abi * s_in_r)], axis=0)
                acc_r = _dot(tril, xr)
                acc_i = _dot(tril, xi)
                lr = acc_r[t - 1:t, :]
                li = acc_i[t - 1:t, :]
                plr = plr_ref[:, sl]
                pli = pli_ref[:, sl]
                st_re[s, :, sl] = lr * plr - li * pli
                st_im[s, :, sl] = lr * pli + li * plr
                ar = acc_r.astype(BF16)
                ai = acc_i.astype(BF16)
                ppr = ppr_ref[:, sl]
                ppi = ppi_ref[:, sl]
                sre[s, rows, sl] = ar * ppr - ai * ppi
                sim[s, rows, sl] = ar * ppi + ai * ppr

    for s in streams:
        ys = [_dot(sre[s, :, slabs[k]], cre_ref[k]) - _dot(sim[s, :, slabs[k]], cim_ref[k])
              for k in range(S5_SLABS)]
        y = jnp.concatenate(ys, axis=-1) + d_ref[...] * us[s].astype(F32)
        v = _gelu_tanh(y)
        v = v * _sigmoid(_dot(v.astype(BF16), wglu_ref[...]) + bglu_ref[...])
        z = uz_ref[s, :, BRANCH_WIDTH:].astype(F32)
        out_ref[s] = (v * _silu(z)).astype(out_ref.dtype)


def _s5_branch(uz, prm, layer, bsz, lp):
    rb = _pick(lp, (384, 256, 128))
    ns = 1
    w2 = 2 * BRANCH_WIDTH
    lb = functools.partial(_layer_block, layer=layer)
    tab = lb((S5_T, S5_STATES))
    rowv = lb((1, S5_STATES))
    wb = lb((S5_SLABS, S5_SLAB_IN, S5_SLAB_ST))
    cb = lb((S5_SLABS, S5_SLAB_ST, S5_SLAB_IN))
    vec = lb((1, BRANCH_WIDTH))
    out = pl.pallas_call(
        functools.partial(_s5_kernel, rb=rb, ns=ns),
        grid=(bsz // ns, lp // rb),
        in_specs=[pl.BlockSpec((ns, rb, w2), lambda b, r: (b, r, 0)),
                  wb, wb, tab, tab, tab, tab, rowv, rowv, rowv, rowv, cb, cb, vec,
                  lb((BRANCH_WIDTH, BRANCH_WIDTH)), vec],
        out_specs=pl.BlockSpec((ns, rb, BRANCH_WIDTH), lambda b, r: (b, r, 0)),
        out_shape=jax.ShapeDtypeStruct((bsz, lp, BRANCH_WIDTH), BF16),
        scratch_shapes=[pltpu.VMEM((ns, rb, S5_STATES), BF16), pltpu.VMEM((ns, rb, S5_STATES), BF16),
                        pltpu.VMEM((ns, rb, S5_STATES), BF16), pltpu.VMEM((ns, rb, S5_STATES), BF16),
                        pltpu.VMEM((ns, 1, S5_STATES), F32), pltpu.VMEM((ns, 1, S5_STATES), F32)],
        compiler_params=_params("parallel", "arbitrary"),
        name="s5_branch",
    )(uz.reshape(bsz, lp, w2), prm["wb_re"], prm["wb_im"], prm["pn_re"], prm["pn_im"], prm["pp_re"], prm["pp_im"],
      prm["ab_re"], prm["ab_im"], prm["pl_re"], prm["pl_im"], prm["c_re"], prm["c_im"],
      prm["d"], prm["w_glu"], prm["b_glu"])
    return out.reshape(bsz * lp, BRANCH_WIDTH)


def _s5_prepare(a_re, a_im, log_step, b_re, b_im, c_re, c_im, d, w_glu, b_glu):
    nl = a_re.shape[0]
    lam_re = jnp.minimum(a_re, -1e-4)
    lam_im = a_im
    step = jnp.exp(log_step)[..., None]
    dec = lam_re * step
    ang = lam_im * step
    mag = jnp.exp(dec)
    abar_re, abar_im = mag * jnp.cos(ang), mag * jnp.sin(ang)
    den = lam_re * lam_re + lam_im * lam_im
    nr, ni = abar_re - 1.0, abar_im
    coef_re = (nr * lam_re + ni * lam_im) / den
    coef_im = (ni * lam_re - nr * lam_im) / den
    bbar_re = coef_re[..., None] * b_re - coef_im[..., None] * b_im
    bbar_im = coef_re[..., None] * b_im + coef_im[..., None] * b_re

    gps = S5_SLAB_IN // S5_GROUP
    eye = jnp.eye(gps, dtype=F32)

    def in_slabs(bb):
        w = jnp.einsum('lkgpc,gh->lkgchp', bb.reshape(nl, S5_SLABS, gps, S5_STATE, S5_GROUP), eye)
        return w.reshape(nl, S5_SLABS, S5_SLAB_IN, S5_SLAB_ST).astype(BF16)

    def out_slabs(cc):
        w = jnp.einsum('lkgcp,gh->lkgphc', cc.reshape(nl, S5_SLABS, gps, S5_GROUP, S5_STATE), eye)
        return w.reshape(nl, S5_SLABS, S5_SLAB_ST, S5_SLAB_IN).astype(BF16)

    idx = jnp.arange(S5_T, dtype=F32)[None, :, None]
    dec = dec.reshape(nl, 1, S5_STATES)
    ang = ang.reshape(nl, 1, S5_STATES)
    cs, sn = jnp.cos(idx * ang), jnp.sin(idx * ang)
    grow, shrink = jnp.exp(idx * dec), jnp.exp(-idx * dec)
    pp_re, pp_im = grow * cs, grow * sn
    return dict(wb_re=in_slabs(bbar_re), wb_im=in_slabs(bbar_im),
                pn_re=(shrink * cs).astype(BF16), pn_im=(-shrink * sn).astype(BF16),
                pp_re=pp_re.astype(BF16), pp_im=pp_im.astype(BF16),
                ab_re=abar_re.reshape(nl, 1, S5_STATES), ab_im=abar_im.reshape(nl, 1, S5_STATES),
                pl_re=pp_re[:, S5_T - 1:], pl_im=pp_im[:, S5_T - 1:],
                c_re=out_slabs(c_re), c_im=out_slabs(c_im), d=d.reshape(nl, 1, -1),
                w_glu=w_glu.astype(BF16), b_glu=b_glu.reshape(nl, 1, -1))


def _conv_begin(buf, carry, x_ref, width, first):
    rb = x_ref.shape[0]

    @pl.when(first)
    def _():
        carry[...] = jnp.zeros_like(carry)

    buf[pl.ds(0, CONV_CARRY), :] = carry[...]
    buf[pl.ds(CONV_CARRY, rb), :] = x_ref[:, :width]
    carry[...] = x_ref[rb - CONV_CARRY:rb, :width]


def _causal_conv(buf, c, x, shift_ref, w, q):
    taps = _dot(shift_ref[...], buf[pl.ds(c * q, q + CONV_CARRY), :])
    acc = x.astype(F32) * w[CONV_K - 1:CONV_K, :]
    for j in range(CONV_K - 1):
        acc = acc + taps[j * q:(j + 1) * q, :] * w[j:j + 1, :]
    return acc


def _tri_masks(q):
    row = lax.broadcasted_iota(jnp.int32, (q, q), 0)
    col = lax.broadcasted_iota(jnp.int32, (q, q), 1)
    return row >= col, row > col, row == col


def _ssd_kernel(xz_ref, sm_ref, shift_ref, cw_ref, cb_ref, dtb_ref, nega_ref, e64_ref, e128_ref,
                d_ref, ng_ref, out_ref, buf, carry, state, *, q, nc):
    first = pl.program_id(1) == 0

    @pl.when(first)
    def _():
        state[...] = jnp.zeros_like(state)

    _conv_begin(buf, carry, xz_ref, SSD_CONV_WIDTH, first)

    causal, _, _ = _tri_masks(q)
    tril = jnp.where(causal, 1.0, 0.0).astype(BF16)
    low_half = lax.broadcasted_iota(jnp.int32, (q, LANES), 1) < SSD_HEAD_DIM

    pre = []
    for c in range(nc):
        rows = slice(c * q, (c + 1) * q)
        xbc = _silu(_causal_conv(buf, c, xz_ref[rows, :SSD_CONV_WIDTH], shift_ref, cw_ref[...], q) + cb_ref[...])
        xs = xbc[:, :BRANCH_WIDTH]
        bmat = xbc[:, BRANCH_WIDTH:BRANCH_WIDTH + SSD_BC]
        cmat = xbc[:, BRANCH_WIDTH + SSD_BC:]
        dt = _softplus(sm_ref[rows, :] + dtb_ref[...])
        da = dt * nega_ref[...]
        acum = _cumsum_rows(tril, da)
        acum_t = acum.T
        a3 = _split3(acum)
        acum64 = _dot(a3, e64_ref[...])
        acum128 = _dot(a3, e128_ref[...])
        dt64 = _dot(_split3(dt), e64_ref[...])
        alast = acum64[q - 1:q, :]
        xd = xs * dt64
        bgs = [bmat[:, g * SSD_STATE:(g + 1) * SSD_STATE] for g in range(SSD_GROUPS)]
        cgs = [cmat[:, g * SSD_STATE:(g + 1) * SSD_STATE].astype(BF16) for g in range(SSD_GROUPS)]
        scores = [_dot_nt(cgs[g], bgs[g].astype(BF16)) for g in range(SSD_GROUPS)]
        ms, rhs = [], []
        for h in range(SSD_HEADS):
            lmat = jnp.where(causal, jnp.exp(acum128[:, h * LANES:(h + 1) * LANES] - acum_t[h:h + 1, :]), 0.0)
            ms.append((scores[h // SSD_HPG] * lmat).astype(BF16))
            pair = xd[:, (h // 2) * LANES:(h // 2 + 1) * LANES]
            keep = low_half if h % 2 == 0 else jnp.logical_not(low_half)
            rhs.append(jnp.where(keep, pair, 0.0).astype(BF16))
        ypairs = [_dot(ms[2 * p], rhs[2 * p]) + _dot(ms[2 * p + 1], rhs[2 * p + 1])
                  for p in range(SSD_HEADS // 2)]
        pre.append(dict(
            y=jnp.concatenate(ypairs, axis=-1) + xs * d_ref[...], cgs=cgs,
            bts=[bgs[g].T.astype(BF16) for g in range(SSD_GROUPS)],
            e_in=jnp.exp(acum64),
            xdw=(xd * jnp.exp(alast - acum64)).astype(BF16),
            e_all=jnp.exp(alast)))

    sts = [state[g] for g in range(SSD_GROUPS)]
    for c in range(nc):
        rows = slice(c * q, (c + 1) * q)
        pc = pre[c]
        y_off = jnp.concatenate([_dot(pc["cgs"][g], sts[g].astype(BF16)) for g in range(SSD_GROUPS)], axis=-1)
        for g in range(SSD_GROUPS):
            gs = slice(g * SSD_GW, (g + 1) * SSD_GW)
            sts[g] = sts[g] * pc["e_all"][:, gs] + _dot(pc["bts"][g], pc["xdw"][:, gs])
        y = (pc["y"] + y_off * pc["e_in"]) * _silu(xz_ref[rows, SSD_CONV_WIDTH:].astype(F32))
        y = y * lax.rsqrt(jnp.mean(y * y, axis=-1, keepdims=True) + LN_EPS) * ng_ref[...]
        out_ref[rows, :] = y.astype(out_ref.dtype)
    for g in range(SSD_GROUPS):
        state[g] = sts[g]


def _ssd_branch(xz, small, prm, layer, bsz, lp):
    q = CHUNK
    rb = _pick(lp, (3 * q, 2 * q, q))
    nblk = lp // rb
    wx = SSD_CONV_WIDTH + BRANCH_WIDTH
    rowmap = lambda b, r: (b * nblk + r, 0)
    lb = functools.partial(_layer_block, layer=layer)
    shift = _shift_matrix(q)
    e64 = _expand_matrix(SM_SSD_DT, SSD_HEADS, SSD_HEAD_DIM, 3)
    e128 = _expand_matrix(SM_SSD_DT, SSD_HEADS, LANES, 3)
    return pl.pallas_call(
        functools.partial(_ssd_kernel, q=q, nc=rb // q),
        grid=(bsz, nblk),
        in_specs=[pl.BlockSpec((rb, wx), rowmap), pl.BlockSpec((rb, LANES), rowmap),
                  _full(shift.shape), lb((CONV_K, SSD_CONV_WIDTH)), lb((1, SSD_CONV_WIDTH)),
                  lb((1, LANES)), lb((1, LANES)), _full(e64.shape), _full(e128.shape),
                  lb((1, BRANCH_WIDTH)), lb((1, BRANCH_WIDTH))],
        out_specs=pl.BlockSpec((rb, BRANCH_WIDTH), rowmap),
        out_shape=jax.ShapeDtypeStruct((bsz * lp, BRANCH_WIDTH), BF16),
        scratch_shapes=[pltpu.VMEM((rb + CONV_CARRY, SSD_CONV_WIDTH), BF16),
                        pltpu.VMEM((CONV_CARRY, SSD_CONV_WIDTH), BF16),
                        pltpu.VMEM((SSD_GROUPS, SSD_STATE, SSD_GW), F32)],
        compiler_params=_params("parallel", "arbitrary"),
        name="ssd_branch",
    )(xz, small, shift, prm["conv_w"], prm["conv_b"], prm["dt_bias"], prm["neg_a"], e64, e128,
      prm["d"], prm["norm_g"])


def _lane_vectors(values, offset):
    nl, n = values.shape
    return jnp.pad(values.astype(F32), ((0, 0), (offset, LANES - offset - n))).reshape(nl, 1, LANES)


def _ssd_prepare(conv_w, conv_b, dt_bias, a_log, d, norm_g):
    nl = conv_w.shape[0]
    return dict(conv_w=conv_w, conv_b=conv_b.reshape(nl, 1, -1),
                dt_bias=_lane_vectors(dt_bias, SM_SSD_DT),
                neg_a=_lane_vectors(-jnp.exp(a_log), SM_SSD_DT),
                d=jnp.repeat(d, SSD_HEAD_DIM, axis=1).reshape(nl, 1, -1), norm_g=norm_g.reshape(nl, 1, -1))


def _head_sums(x, ones_ref):
    outs = []
    for s in range(x.shape[1] // (2 * LANES)):
        xs = x[:, s * 2 * LANES:(s + 1) * 2 * LANES]
        outs.append(_dot(xs.astype(BF16), ones_ref[...]))
    return jnp.concatenate(outs, axis=-1)


def _gdn_kernel(qkvz_ref, sm_ref, shift_ref, cw_ref, dtb_ref, nega_ref, eg_ref, eb_ref, ones_ref, ng_ref,
                out_ref, buf, carry, state, *, q, nc, ns, lr):
    first = pl.program_id(1) == 0

    @pl.when(first)
    def _():
        state[...] = jnp.zeros_like(state)

    for s in range(ns):
        _conv_begin(buf.at[s], carry.at[s], qkvz_ref.at[s], GDN_QKV, first)
    cw = cw_ref[...]

    causal, strict, _ = _tri_masks(q)
    tril = jnp.where(causal, 1.0, 0.0).astype(BF16)

    pre = {}
    for s, c in [(s, c) for s in range(ns) for c in range(nc)]:
        rows = slice(c * q, (c + 1) * q)
        qkv = _silu(_causal_conv(buf.at[s], c, qkvz_ref[s, rows, :GDN_QKV], shift_ref, cw, q))
        qa = qkv[:, :BRANCH_WIDTH]
        ka = qkv[:, BRANCH_WIDTH:2 * BRANCH_WIDTH]
        va = qkv[:, 2 * BRANCH_WIDTH:]
        qa = qa * (lax.rsqrt(_head_sums(qa * qa, ones_ref) + 1e-6) * (GDN_HEAD_DIM ** -0.5))
        ka = ka * lax.rsqrt(_head_sums(ka * ka, ones_ref) + 1e-6)
        sm = sm_ref[s, rows, :]
        glog = nega_ref[...] * _softplus(sm + dtb_ref[...])
        gcum = _cumsum_rows(tril, glog)
        g128 = _dot(_split3(gcum), eg_ref[...])
        beta = _dot(_split2(_sigmoid(sm)), eb_ref[...])
        rid = lax.broadcasted_iota(jnp.int32, beta.shape, 0) + (pl.program_id(1) * (nc * q) + c * q)
        beta = jnp.where(rid < lr, beta, 0.0)
        glast = g128[q - 1:q, :]
        e_in = jnp.exp(g128)
        pre[s, c] = dict(qa=qa, ka=ka, beta=beta, g128=g128, gcum_t=gcum.T, e_all=jnp.exp(glast),
                         vbeta=va * beta, kbe=ka * (beta * e_in), qg=qa * e_in, kd=ka * jnp.exp(glast - g128))

    dk = GDN_HEAD_DIM
    npair = GDN_HEADS // 2
    units = [(s, c, p) for s in range(ns) for c in range(nc) for p in range(npair)]
    ps = [slice(p * 2 * dk, (p + 1) * 2 * dk) for p in range(npair)]
    left = lax.broadcasted_iota(jnp.int32, (q, 2 * dk), 1) < dk

    def blockdiag(x):
        return jnp.concatenate([jnp.where(left, x, 0.0), jnp.where(left, 0.0, x)], axis=0).astype(BF16)

    causal2 = jnp.concatenate([causal, causal], axis=1)
    strict2 = jnp.concatenate([strict, strict], axis=1)
    gammas, pws, tinvs, attns = {}, {}, {}, {}
    for s, c, p in units:
        la = SM_GDN_A + 2 * p
        gt = pre[s, c]["gcum_t"]
        grow = jnp.concatenate([gt[la:la + 1, :], gt[la + 1:la + 2, :]], axis=1)
        gammas[s, c, p] = jnp.where(causal2, jnp.exp(pre[s, c]["g128"][:, ps[p]] - grow), 0.0)
    for s, c, p in units:
        kp = pre[s, c]["ka"][:, ps[p]]
        lhs = jnp.concatenate([kp.astype(BF16), pre[s, c]["qa"][:, ps[p]].astype(BF16)], axis=0)
        kq = _dot_nt(lhs, blockdiag(kp))
        pws[s, c, p] = jnp.where(strict2, -(kq[:q] * gammas[s, c, p] * pre[s, c]["beta"][:, ps[p]]), 0.0)
        attns[s, c, p] = (kq[q:] * gammas[s, c, p]).astype(BF16)
    bfull = dict(pws)
    b = GDN_INV_BASE
    per_head = q // b

    def lane_blocks(bs):
        return lax.broadcasted_iota(jnp.int32, (bs, 2 * q), 1) // bs

    blk = lane_blocks(b)
    eyes = jnp.where(lax.broadcasted_iota(jnp.int32, (b, 2 * q), 1) % b
                     == lax.broadcasted_iota(jnp.int32, (b, 2 * q), 0), 1.0, 0.0)

    def blockdiag_b(x):
        return jnp.concatenate([jnp.where(blk == i, x, 0.0) for i in range(2 * q // b)], axis=0).astype(BF16)

    for u in units:
        diag = bfull[u][:b]
        for j in range(1, per_head):
            diag = jnp.where(blk % per_head == j, bfull[u][j * b:(j + 1) * b], diag)
        tinvs[u] = eyes + diag
        pws[u] = _dot(diag.astype(BF16), blockdiag_b(diag))
    steps = int(math.log2(b)) - 1
    for i in range(steps):
        last = i == steps - 1
        for u in units:
            pwb = pws[u].astype(BF16)
            lhs = tinvs[u].astype(BF16) if last else jnp.concatenate([tinvs[u].astype(BF16), pwb], axis=0)
            both = _dot(lhs, blockdiag_b(pws[u]))
            tinvs[u] = tinvs[u] + both[:b]
            if not last:
                pws[u] = both[b:]
    while b < q:
        nblk = 2 * q // b
        blk = lane_blocks(b)
        even = blk % 2 == 0
        zeros = jnp.zeros((b, 2 * q), F32)
        t2b = {}
        for u in units:
            rows = [zeros if i % 2 == 0 else
                    jnp.where(blk == i - 1, bfull[u][(i % (q // b)) * b:(i % (q // b) + 1) * b], 0.0)
                    for i in range(nblk)]
            t2b[u] = _dot(jnp.where(even, 0.0, tinvs[u]).astype(BF16), jnp.concatenate(rows, axis=0).astype(BF16))
        for u in units:
            td = tinvs[u]
            rows = [jnp.where(blk == i, td, 0.0) if i % 2 == 0 else zeros for i in range(nblk)]
            t21 = _dot(t2b[u].astype(BF16), jnp.concatenate(rows, axis=0).astype(BF16))
            tinvs[u] = jnp.concatenate([jnp.where(even, td, 0.0), jnp.where(even, t21, td)], axis=0)
        b *= 2
    us, ws = {}, {}
    for s, c, p in units:
        uw = []
        for i in range(2):
            hcol = slice((2 * p + i) * dk, (2 * p + i + 1) * dk)
            rhs = jnp.concatenate([pre[s, c]["vbeta"][:, hcol], pre[s, c]["kbe"][:, hcol]], axis=-1)
            uw.append(_dot(tinvs[s, c, p][:, i * dk:(i + 1) * dk].astype(BF16), rhs.astype(BF16)))
        us[s, c, p] = jnp.concatenate([uw[0][:, :dk], uw[1][:, :dk]], axis=-1)
        ws[s, c, p] = jnp.concatenate([uw[0][:, dk:], uw[1][:, dk:]], axis=-1)

    sts = {(s, h): state[s, h] for s in range(ns) for h in range(GDN_HEADS)}
    zero = jnp.zeros((dk, dk), F32)
    for c in range(nc):
        rows = slice(c * q, (c + 1) * q)
        wq = {}
        for s in range(ns):
            for p in range(npair):
                sbd = jnp.concatenate([jnp.concatenate([sts[s, 2 * p], zero], axis=1),
                                       jnp.concatenate([zero, sts[s, 2 * p + 1]], axis=1)], axis=0)
                lhs = jnp.concatenate([ws[s, c, p].astype(BF16), pre[s, c]["qg"][:, ps[p]].astype(BF16)], axis=0)
                wq[s, p] = _dot(lhs, sbd.astype(BF16))
        for s in range(ns):
            outs = []
            for p in range(npair):
                vnew = us[s, c, p] - wq[s, p][:q]
                outs.append(wq[s, p][q:] + _dot(attns[s, c, p], blockdiag(vnew)))
                vb = vnew.astype(BF16)
                for i in range(2):
                    h = 2 * p + i
                    hcol = slice(h * dk, (h + 1) * dk)
                    kdt = pre[s, c]["kd"][:, hcol].T.astype(BF16)
                    sts[s, h] = sts[s, h] * pre[s, c]["e_all"][:, hcol] + _dot(kdt, vb[:, i * dk:(i + 1) * dk])
            o = jnp.concatenate(outs, axis=-1)
            o = o * lax.rsqrt(_head_sums(o * o, ones_ref) * (1.0 / dk) + LN_EPS)
            gate = _silu(qkvz_ref[s, rows, GDN_QKV:].astype(F32))
            o = o * jnp.concatenate([ng_ref[...]] * GDN_HEADS, axis=-1) * gate
            out_ref[s, rows, :] = o.astype(out_ref.dtype)
    for s in range(ns):
        for h in range(GDN_HEADS):
            state[s, h] = sts[s, h]


def _gdn_branch(qkvz, small, prm, layer, bsz, lp, lr):
    q = CHUNK
    assert q == GDN_HEAD_DIM
    rb = _pick(lp, (3 * q, 2 * q, q))
    ns = 1
    wx = GDN_QKV + BRANCH_WIDTH
    rowmap = lambda b, r: (b, r, 0)
    lb = functools.partial(_layer_block, layer=layer)
    shift = _shift_matrix(q)
    e_g = _expand_matrix(SM_GDN_A, GDN_HEADS, GDN_HEAD_DIM, 3)
    e_b = _expand_matrix(SM_GDN_B, GDN_HEADS, GDN_HEAD_DIM, 2)
    ones = _block_ones(2 * LANES, GDN_HEAD_DIM, 1)
    out = pl.pallas_call(
        functools.partial(_gdn_kernel, q=q, nc=rb // q, ns=ns, lr=lr),
        grid=(bsz // ns, lp // rb),
        in_specs=[pl.BlockSpec((ns, rb, wx), rowmap), pl.BlockSpec((ns, rb, LANES), rowmap),
                  _full(shift.shape), lb((CONV_K, GDN_QKV)), lb((1, LANES)), lb((1, LANES)),
                  _full(e_g.shape), _full(e_b.shape), _full(ones.shape), lb((1, GDN_HEAD_DIM))],
        out_specs=pl.BlockSpec((ns, rb, BRANCH_WIDTH), rowmap),
        out_shape=jax.ShapeDtypeStruct((bsz, lp, BRANCH_WIDTH), BF16),
        scratch_shapes=[pltpu.VMEM((ns, rb + CONV_CARRY, GDN_QKV), BF16),
                        pltpu.VMEM((ns, CONV_CARRY, GDN_QKV), BF16),
                        pltpu.VMEM((ns, GDN_HEADS, GDN_HEAD_DIM, GDN_HEAD_DIM), F32)],
        compiler_params=_params("parallel", "arbitrary"),
        name="gdn_branch",
    )(qkvz.reshape(bsz, lp, wx), small.reshape(bsz, lp, LANES), shift, prm["conv_w"], prm["dt_bias"],
      prm["neg_a"], e_g, e_b, ones, prm["norm_g"])
    return out.reshape(bsz * lp, BRANCH_WIDTH)


def _gdn_prepare(conv_w, dt_bias, a_log, norm_g):
    nl = conv_w.shape[0]
    return dict(conv_w=conv_w, dt_bias=_lane_vectors(dt_bias, SM_GDN_A),
                neg_a=_lane_vectors(-jnp.exp(a_log), SM_GDN_A), norm_g=norm_g.reshape(nl, 1, -1))


def _merge_kernel(h_ref, ya_ref, yb_ref, yc_ref, wg_ref, bg_ref, wbr_ref, wo_ref, g_ref, b_ref,
                  o_ref, ob_ref, *, alpha):
    h = h_ref[...]
    logits = _dot(h.astype(BF16), wg_ref[...]) + bg_ref[...]
    merged = None
    for n, y_ref in enumerate((ya_ref, yb_ref, yc_ref)):
        gate = _sigmoid(logits[:, n * D_MODEL:(n + 1) * D_MODEL])
        term = gate * _dot(y_ref[...], wbr_ref[n])
        merged = term if merged is None else merged + term
    y = alpha * h + _dot(merged.astype(BF16), wo_ref[...])
    y = _ln_math(y, g_ref[...], b_ref[...])
    o_ref[...] = y
    ob_ref[...] = y.astype(BF16)


def _merge(h, ya, yb, yc, prm, layer, alpha):
    m, d = h.shape
    tm = _pick(m, (512, 256, 128))
    rows = lambda w: pl.BlockSpec((tm, w), lambda i: (i, 0))
    lb = functools.partial(_layer_block, layer=layer)
    return pl.pallas_call(
        functools.partial(_merge_kernel, alpha=alpha),
        grid=(m // tm,),
        in_specs=[rows(d), rows(BRANCH_WIDTH), rows(BRANCH_WIDTH), rows(BRANCH_WIDTH),
                  lb((d, N_BRANCH * d)), lb((1, N_BRANCH * d)),
                  lb((N_BRANCH, BRANCH_WIDTH, d)), lb((d, d)), lb((1, d)), lb((1, d))],
        out_specs=[rows(d), rows(d)],
        out_shape=[jax.ShapeDtypeStruct((m, d), F32), jax.ShapeDtypeStruct((m, d), BF16)],
        compiler_params=_params("parallel"),
        name="merge",
    )(h, ya, yb, yc, prm["w_gate"], prm["b_gate"], prm["w_branch"], prm["w_out"], prm["ln_g"], prm["ln_b"])


def _split_in_proj(w):
    widths = (BRANCH_WIDTH, BRANCH_WIDTH, SSD_CONV_WIDTH, SSD_HEADS, BRANCH_WIDTH,
              GDN_QKV, GDN_HEADS, GDN_HEADS, BRANCH_WIDTH, N_BRANCH * D_MODEL)
    offs = [0]
    for wd in widths:
        offs.append(offs[-1] + wd)
    seg = [w[:, :, offs[i]:offs[i + 1]] for i in range(len(widths))]
    s5_u, s5_z, ssd_xbc, ssd_dt, ssd_z, gdn_qkv, gdn_a, gdn_b, gdn_z, gate = seg
    pad = jnp.zeros(w.shape[:2] + (LANES - SSD_HEADS - 2 * GDN_HEADS,), w.dtype)
    return dict(
        s5=jnp.concatenate([s5_u, s5_z], axis=2).astype(BF16),
        ssd=jnp.concatenate([ssd_xbc, ssd_z], axis=2).astype(BF16),
        gdn=jnp.concatenate([gdn_qkv, gdn_z], axis=2).astype(BF16),
        small=jnp.concatenate([ssd_dt, gdn_a, gdn_b, pad], axis=2).astype(BF16),
        gate=gate.astype(BF16))


def kernel(x, meta, ln_in_g, ln_in_b, w_in, s5_a_re, s5_a_im, s5_log_step, s5_b_re, s5_b_im, s5_c_re, s5_c_im, s5_d, s5_w_glu, s5_b_glu, ssd_conv_w, ssd_conv_b, ssd_dt_bias, ssd_a_log, ssd_d, ssd_norm_g, gdn_conv_w, gdn_dt_bias, gdn_a_log, gdn_norm_g, w_branch, b_gate, w_out, ln_g, ln_b):
    bsz, seq, d = x.shape
    depth = w_in.shape[0]
    alpha = (2 * depth) ** 0.25
    lr = N_META + seq
    lp = -(-lr // CHUNK) * CHUNK
    h0 = jnp.concatenate([jnp.broadcast_to(meta[None].astype(x.dtype), (bsz, N_META, d)), x,
                          jnp.zeros((bsz, lp - lr, d), x.dtype)], axis=1).reshape(bsz * lp, d)
    h, hb = _layer_norm_in(h0, ln_in_g, ln_in_b)

    wi = _split_in_proj(w_in)
    s5p = _s5_prepare(s5_a_re, s5_a_im, s5_log_step, s5_b_re, s5_b_im, s5_c_re, s5_c_im, s5_d, s5_w_glu, s5_b_glu)
    ssdp = _ssd_prepare(ssd_conv_w, ssd_conv_b, ssd_dt_bias, ssd_a_log, ssd_d, ssd_norm_g)
    gdnp = _gdn_prepare(gdn_conv_w, gdn_dt_bias, gdn_a_log, gdn_norm_g)
    mrg = dict(w_gate=wi["gate"], b_gate=b_gate.reshape(depth, 1, -1), w_branch=w_branch.astype(BF16),
               w_out=w_out.astype(BF16), ln_g=ln_g.reshape(depth, 1, -1), ln_b=ln_b.reshape(depth, 1, -1))

    for layer in range(depth):
        p_s5, p_ssd, p_gdn, small = _in_proj(hb, [wi["s5"], wi["ssd"], wi["gdn"], wi["small"]], layer,
                                             [BF16, BF16, BF16, F32])
        y_a = _s5_branch(p_s5, s5p, layer, bsz, lp)
        y_b = _ssd_branch(p_ssd, small, ssdp, layer, bsz, lp)
        y_c = _gdn_branch(p_gdn, small, gdnp, layer, bsz, lp, lr)
        h, hb = _merge(h, y_a, y_b, y_c, mrg, layer, alpha)

    return h.reshape(bsz, lp, d)[:, N_META:lr]
```

```python
import functools
import math

import numpy as np
import jax
import jax.numpy as jnp
from jax import lax
from jax.experimental import pallas as pl
from jax.experimental.pallas import tpu as pltpu

D_MODEL = 1024
N_META = 16
CONV_K = 4
N_BRANCH = 3
BRANCH_WIDTH = 768

S5_GROUP = 16
S5_GROUPS = BRANCH_WIDTH // S5_GROUP
S5_STATE = 64
S5_STATES = S5_GROUPS * S5_STATE
S5_SLAB_IN = 256
S5_SLAB_ST = 1024
S5_SLABS = BRANCH_WIDTH // S5_SLAB_IN
S5_T = 128
S5_CARRY_ROWS = 16

SSD_HEAD_DIM = 64
SSD_HEADS = 12
SSD_GROUPS = 2
SSD_HPG = SSD_HEADS // SSD_GROUPS
SSD_STATE = 128
SSD_BC = SSD_GROUPS * SSD_STATE
SSD_CONV_WIDTH = BRANCH_WIDTH + 2 * SSD_BC
SSD_GW = SSD_HPG * SSD_HEAD_DIM

GDN_HEAD_DIM = 128
GDN_HEADS = 6
GDN_QKV = 3 * BRANCH_WIDTH
GDN_INV_BASE = 64

CHUNK = 128
LANES = 128
CONV_CARRY = 16
LN_EPS = 1e-5

SM_SSD_DT = 0
SM_GDN_A = SSD_HEADS
SM_GDN_B = SSD_HEADS + GDN_HEADS

VMEM_LIMIT = 56 * 1024 * 1024

F32 = jnp.float32
BF16 = jnp.bfloat16


def _dot(a, b):
    return jnp.dot(a, b, preferred_element_type=F32)


def _cumsum_rows(tril, x):
    parts = _dot(tril, _split3(x))
    return parts[:, :LANES] + parts[:, LANES:2 * LANES] + parts[:, 2 * LANES:]


def _dot_nt(a, b):
    return lax.dot_general(a, b, (((1,), (1,)), ((), ())), preferred_element_type=F32)


def _sigmoid(x):
    return 1.0 / (1.0 + jnp.exp(-x))


def _silu(x):
    return x * _sigmoid(x)


def _softplus(x):
    return jnp.maximum(x, 0.0) + jnp.log1p(jnp.exp(-jnp.abs(x)))


def _gelu_tanh(x):
    c = math.sqrt(2.0 / math.pi)
    return 0.5 * x * (1.0 + jnp.tanh(c * (x + 0.044715 * (x * x * x))))


def _bf16_floor(x):
    bits = lax.bitcast_convert_type(x, jnp.uint32) & jnp.uint32(0xFFFF0000)
    return lax.bitcast_convert_type(bits, F32)


def _split3(x):
    hi = _bf16_floor(x)
    r1 = x - hi
    mid = _bf16_floor(r1)
    return jnp.concatenate([hi.astype(BF16), mid.astype(BF16), (r1 - mid).astype(BF16)], axis=-1)


def _split2(x):
    hi = _bf16_floor(x)
    return jnp.concatenate([hi.astype(BF16), (x - hi).astype(BF16)], axis=-1)


def _pick(n, candidates):
    for c in candidates:
        if n % c == 0:
            return c
    raise ValueError(f"no tile for {n} in {candidates}")


def _params(*semantics):
    return pltpu.CompilerParams(dimension_semantics=semantics, vmem_limit_bytes=VMEM_LIMIT)


def _full(shape):
    n = len(shape)
    return pl.BlockSpec(shape, lambda *_: (0,) * n)


def _layer_block(shape, layer):
    n = len(shape)
    return pl.BlockSpec((None,) + tuple(shape), lambda *_: (layer,) + (0,) * n)


def _expand_matrix(first_lane, heads, width, pieces):
    e = np.zeros((LANES, heads * width), np.float32)
    for h in range(heads):
        e[first_lane + h, h * width:(h + 1) * width] = 1.0
    return jnp.asarray(np.tile(e, (pieces, 1)), BF16)


def _shift_matrix(q):
    s = np.zeros(((CONV_K - 1) * q, q + CONV_CARRY), np.float32)
    for j in range(CONV_K - 1):
        for t in range(q):
            s[j * q + t, CONV_CARRY + t - (CONV_K - 1) + j] = 1.0
    return jnp.asarray(s, BF16)


def _block_ones(width, block, pieces):
    o = np.kron(np.eye(width // block, dtype=np.float32), np.ones((block, block), np.float32))
    return jnp.asarray(np.tile(o, (pieces, 1)), BF16)


def _ln_math(x, g, b):
    mu = jnp.mean(x, axis=-1, keepdims=True)
    xc = x - mu
    var = jnp.mean(xc * xc, axis=-1, keepdims=True)
    return xc * lax.rsqrt(var + LN_EPS) * g + b


def _ln_kernel(x_ref, g_ref, b_ref, o_ref, ob_ref):
    y = _ln_math(x_ref[...], g_ref[...], b_ref[...])
    o_ref[...] = y
    ob_ref[...] = y.astype(BF16)


def _layer_norm_in(x2d, g, b):
    m, d = x2d.shape
    tm = _pick(m, (512, 256, 128))
    return pl.pallas_call(
        _ln_kernel,
        grid=(m // tm,),
        in_specs=[pl.BlockSpec((tm, d), lambda i: (i, 0)), _full((1, d)), _full((1, d))],
        out_specs=[pl.BlockSpec((tm, d), lambda i: (i, 0)), pl.BlockSpec((tm, d), lambda i: (i, 0))],
        out_shape=[jax.ShapeDtypeStruct((m, d), F32), jax.ShapeDtypeStruct((m, d), BF16)],
        compiler_params=_params("parallel"),
        name="ln_in",
    )(x2d, g.reshape(1, d), b.reshape(1, d))


def _in_proj_kernel(x_ref, *refs):
    n = len(refs) // 2
    x = x_ref[...]
    for w_ref, o_ref in zip(refs[:n], refs[n:]):
        o_ref[...] = _dot(x, w_ref[...]).astype(o_ref.dtype)


def _in_proj(x, weights, layer, out_dtypes):
    m, k = x.shape
    tm = _pick(m, (768, 512, 256, 128))
    widths = [w.shape[-1] for w in weights]
    rows = lambda n: pl.BlockSpec((tm, n), lambda i: (i, 0))
    return pl.pallas_call(
        _in_proj_kernel,
        grid=(m // tm,),
        in_specs=[rows(k)] + [_layer_block((k, n), layer) for n in widths],
        out_specs=[rows(n) for n in widths],
        out_shape=[jax.ShapeDtypeStruct((m, n), dt) for n, dt in zip(widths, out_dtypes)],
        compiler_params=_params("parallel"),
        name="in_proj",
    )(x, *weights)


def _carry_rows(c):
    hi = _bf16_floor(c)
    r1 = c - hi
    mid = _bf16_floor(r1)
    lo = r1 - mid
    rid = lax.broadcasted_iota(jnp.int32, (S5_CARRY_ROWS, c.shape[1]), 0)
    ext = jnp.where(rid == 0, hi, jnp.where(rid == 1, mid, jnp.where(rid == 2, lo, 0.0)))
    return ext.astype(BF16)


def _s5_kernel(uz_ref, wbre_ref, wbim_ref, pnr_ref, pni_ref, ppr_ref, ppi_ref, abr_ref, abi_ref,
               plr_ref, pli_ref, cre_ref, cim_ref, d_ref, wglu_ref, bglu_ref, out_ref,
               bure, buim, sre, sim, st_re, st_im, *, rb, ns):
    @pl.when(pl.program_id(1) == 0)
    def _():
        st_re[...] = jnp.zeros_like(st_re)
        st_im[...] = jnp.zeros_like(st_im)

    streams = range(ns)
    slabs = [slice(k * S5_SLAB_ST, (k + 1) * S5_SLAB_ST) for k in range(S5_SLABS)]
    us = [uz_ref[s, :, :BRANCH_WIDTH] for s in streams]
    for s in streams:
        for k in range(S5_SLABS):
            uk = us[s][:, k * S5_SLAB_IN:(k + 1) * S5_SLAB_IN]
            bure[s, :, slabs[k]] = _dot(uk, wbre_ref[k]).astype(BF16)
            buim[s, :, slabs[k]] = _dot(uk, wbim_ref[k]).astype(BF16)

    t = S5_T
    row = lax.broadcasted_iota(jnp.int32, (t, t + S5_CARRY_ROWS), 0)
    col = lax.broadcasted_iota(jnp.int32, (t, t + S5_CARRY_ROWS), 1)
    ones_cols = jnp.where(col >= t, 1.0, 0.0) * jnp.where(col < t + 3, 1.0, 0.0)
    tril = (jnp.where(row >= col, 1.0, 0.0) + ones_cols).astype(BF16)

    for c in range(rb // t):
        rows = slice(c * t, (c + 1) * t)
        for s in streams:
            for sl in slabs:
                br = bure[s, rows, sl]
                bi = buim[s, rows, sl]
                pnr = pnr_ref[:, sl]
                pni = pni_ref[:, sl]
                s_in_r = st_re[s, :, sl]
                s_in_i = st_im[s, :, sl]
                abr = abr_ref[:, sl]
                abi = abi_ref[:, sl]
                xr = jnp.concatenate([br * pnr - bi * pni, _carry_rows(abr * s_in_r - abi * s_in_i)], axis=0)
                xi = jnp.concatenate([br * pni + bi * pnr, _carry_rows(abr * s_in_i + abi * s_in_r)], axis=0)
                acc_r = _dot(tril, xr)
                acc_i = _dot(tril, xi)
                lr = acc_r[t - 1:t, :]
                li = acc_i[t - 1:t, :]
                plr = plr_ref[:, sl]
                pli = pli_ref[:, sl]
                st_re[s, :, sl] = lr * plr - li * pli
                st_im[s, :, sl] = lr * pli + li * plr
                ar = acc_r.astype(BF16)
                ai = acc_i.astype(BF16)
                ppr = ppr_ref[:, sl]
                ppi = ppi_ref[:, sl]
                sre[s, rows, sl] = ar * ppr - ai * ppi
                sim[s, rows, sl] = ar * ppi + ai * ppr

    for s in streams:
        ys = [_dot(sre[s, :, slabs[k]], cre_ref[k]) - _dot(sim[s, :, slabs[k]], cim_ref[k])
              for k in range(S5_SLABS)]
        y = jnp.concatenate(ys, axis=-1) + d_ref[...] * us[s].astype(F32)
        v = _gelu_tanh(y)
        v = v * _sigmoid(_dot(v.astype(BF16), wglu_ref[...]) + bglu_ref[...])
        z = uz_ref[s, :, BRANCH_WIDTH:].astype(F32)
        out_ref[s] = (v * _silu(z)).astype(out_ref.dtype)


def _s5_branch(uz, prm, layer, bsz, lp):
    rb = _pick(lp, (384, 256, 128))
    ns = 1
    w2 = 2 * BRANCH_WIDTH
    lb = functools.partial(_layer_block, layer=layer)
    tab = lb((S5_T, S5_STATES))
    rowv = lb((1, S5_STATES))
    wb = lb((S5_SLABS, S5_SLAB_IN, S5_SLAB_ST))
    cb = lb((S5_SLABS, S5_SLAB_ST, S5_SLAB_IN))
    vec = lb((1, BRANCH_WIDTH))
    out = pl.pallas_call(
        functools.partial(_s5_kernel, rb=rb, ns=ns),
        grid=(bsz // ns, lp // rb),
        in_specs=[pl.BlockSpec((ns, rb, w2), lambda b, r: (b, r, 0)),
                  wb, wb, tab, tab, tab, tab, rowv, rowv, rowv, rowv, cb, cb, vec,
                  lb((BRANCH_WIDTH, BRANCH_WIDTH)), vec],
        out_specs=pl.BlockSpec((ns, rb, BRANCH_WIDTH), lambda b, r: (b, r, 0)),
        out_shape=jax.ShapeDtypeStruct((bsz, lp, BRANCH_WIDTH), BF16),
        scratch_shapes=[pltpu.VMEM((ns, rb, S5_STATES), BF16), pltpu.VMEM((ns, rb, S5_STATES), BF16),
                        pltpu.VMEM((ns, rb, S5_STATES), BF16), pltpu.VMEM((ns, rb, S5_STATES), BF16),
                        pltpu.VMEM((ns, 1, S5_STATES), F32), pltpu.VMEM((ns, 1, S5_STATES), F32)],
        compiler_params=_params("parallel", "arbitrary"),
        name="s5_branch",
    )(uz.reshape(bsz, lp, w2), prm["wb_re"], prm["wb_im"], prm["pn_re"], prm["pn_im"], prm["pp_re"], prm["pp_im"],
      prm["ab_re"], prm["ab_im"], prm["pl_re"], prm["pl_im"], prm["c_re"], prm["c_im"],
      prm["d"], prm["w_glu"], prm["b_glu"])
    return out.reshape(bsz * lp, BRANCH_WIDTH)


def _s5_prepare(a_re, a_im, log_step, b_re, b_im, c_re, c_im, d, w_glu, b_glu):
    nl = a_re.shape[0]
    lam_re = jnp.minimum(a_re, -1e-4)
    lam_im = a_im
    step = jnp.exp(log_step)[..., None]
    dec = lam_re * step
    ang = lam_im * step
    mag = jnp.exp(dec)
    abar_re, abar_im = mag * jnp.cos(ang), mag * jnp.sin(ang)
    den = lam_re * lam_re + lam_im * lam_im
    nr, ni = abar_re - 1.0, abar_im
    coef_re = (nr * lam_re + ni * lam_im) / den
    coef_im = (ni * lam_re - nr * lam_im) / den
    bbar_re = coef_re[..., None] * b_re - coef_im[..., None] * b_im
    bbar_im = coef_re[..., None] * b_im + coef_im[..., None] * b_re

    gps = S5_SLAB_IN // S5_GROUP
    eye = jnp.eye(gps, dtype=BF16)

    def in_slabs(bb):
        t = bb.astype(BF16).reshape(nl, S5_SLABS, gps, S5_STATE, S5_GROUP).transpose(0, 1, 2, 4, 3)
        w = t[:, :, :, :, None, :] * eye[None, None, :, None, :, None]
        return w.reshape(nl, S5_SLABS, S5_SLAB_IN, S5_SLAB_ST)

    def out_slabs(cc):
        t = cc.astype(BF16).reshape(nl, S5_SLABS, gps, S5_GROUP, S5_STATE).transpose(0, 1, 2, 4, 3)
        w = t[:, :, :, :, None, :] * eye[None, None, :, None, :, None]
        return w.reshape(nl, S5_SLABS, S5_SLAB_ST, S5_SLAB_IN)

    idx = jnp.arange(S5_T, dtype=F32)[None, :, None]
    dec = dec.reshape(nl, 1, S5_STATES)
    ang = ang.reshape(nl, 1, S5_STATES)
    cs, sn = jnp.cos(idx * ang), jnp.sin(idx * ang)
    grow, shrink = jnp.exp(idx * dec), jnp.exp(-idx * dec)
    pp_re, pp_im = grow * cs, grow * sn
    return dict(wb_re=in_slabs(bbar_re), wb_im=in_slabs(bbar_im),
                pn_re=(shrink * cs).astype(BF16), pn_im=(-shrink * sn).astype(BF16),
                pp_re=pp_re.astype(BF16), pp_im=pp_im.astype(BF16),
                ab_re=abar_re.reshape(nl, 1, S5_STATES), ab_im=abar_im.reshape(nl, 1, S5_STATES),
                pl_re=pp_re[:, S5_T - 1:], pl_im=pp_im[:, S5_T - 1:],
                c_re=out_slabs(c_re), c_im=out_slabs(c_im), d=d.reshape(nl, 1, -1),
                w_glu=w_glu.astype(BF16), b_glu=b_glu.reshape(nl, 1, -1))


def _conv_begin(buf, carry, x_ref, width, first):
    rb = x_ref.shape[0]

    @pl.when(first)
    def _():
        carry[...] = jnp.zeros_like(carry)

    buf[pl.ds(0, CONV_CARRY), :] = carry[...]
    buf[pl.ds(CONV_CARRY, rb), :] = x_ref[:, :width]
    carry[...] = x_ref[rb - CONV_CARRY:rb, :width]


def _causal_conv(buf, c, x, shift_ref, w, q):
    taps = _dot(shift_ref[...], buf[pl.ds(c * q, q + CONV_CARRY), :])
    acc = x.astype(F32) * w[CONV_K - 1:CONV_K, :]
    for j in range(CONV_K - 1):
        acc = acc + taps[j * q:(j + 1) * q, :] * w[j:j + 1, :]
    return acc


def _tri_masks(q):
    row = lax.broadcasted_iota(jnp.int32, (q, q), 0)
    col = lax.broadcasted_iota(jnp.int32, (q, q), 1)
    return row >= col, row > col, row == col


def _ssd_kernel(xz_ref, sm_ref, shift_ref, cw_ref, cb_ref, dtb_ref, nega_ref, e64_ref, e128_ref,
                d_ref, ng_ref, out_ref, buf, carry, state, *, q, nc):
    first = pl.program_id(1) == 0

    @pl.when(first)
    def _():
        state[...] = jnp.zeros_like(state)

    _conv_begin(buf, carry, xz_ref, SSD_CONV_WIDTH, first)

    causal, _, _ = _tri_masks(q)
    tril = jnp.where(causal, 1.0, 0.0).astype(BF16)
    low_half = lax.broadcasted_iota(jnp.int32, (q, LANES), 1) < SSD_HEAD_DIM

    pre = []
    for c in range(nc):
        rows = slice(c * q, (c + 1) * q)
        xbc = _silu(_causal_conv(buf, c, xz_ref[rows, :SSD_CONV_WIDTH], shift_ref, cw_ref[...], q) + cb_ref[...])
        xs = xbc[:, :BRANCH_WIDTH]
        bmat = xbc[:, BRANCH_WIDTH:BRANCH_WIDTH + SSD_BC]
        cmat = xbc[:, BRANCH_WIDTH + SSD_BC:]
        dt = _softplus(sm_ref[rows, :] + dtb_ref[...])
        da = dt * nega_ref[...]
        acum = _cumsum_rows(tril, da)
        acum_t = acum.T
        a3 = _split3(acum)
        acum64 = _dot(a3, e64_ref[...])
        acum128 = _dot(a3, e128_ref[...])
        dt64 = _dot(_split3(dt), e64_ref[...])
        alast = acum64[q - 1:q, :]
        xd = xs * dt64
        bgs = [bmat[:, g * SSD_STATE:(g + 1) * SSD_STATE] for g in range(SSD_GROUPS)]
        cgs = [cmat[:, g * SSD_STATE:(g + 1) * SSD_STATE].astype(BF16) for g in range(SSD_GROUPS)]
        scores = [_dot_nt(cgs[g], bgs[g].astype(BF16)) for g in range(SSD_GROUPS)]
        ms, rhs = [], []
        for h in range(SSD_HEADS):
            lmat = jnp.where(causal, jnp.exp(acum128[:, h * LANES:(h + 1) * LANES] - acum_t[h:h + 1, :]), 0.0)
            ms.append((scores[h // SSD_HPG] * lmat).astype(BF16))
            pair = xd[:, (h // 2) * LANES:(h // 2 + 1) * LANES]
            keep = low_half if h % 2 == 0 else jnp.logical_not(low_half)
            rhs.append(jnp.where(keep, pair, 0.0).astype(BF16))
        ypairs = [_dot(ms[2 * p], rhs[2 * p]) + _dot(ms[2 * p + 1], rhs[2 * p + 1])
                  for p in range(SSD_HEADS // 2)]
        pre.append(dict(
            y=jnp.concatenate(ypairs, axis=-1) + xs * d_ref[...], cgs=cgs,
            bts=[bgs[g].T.astype(BF16) for g in range(SSD_GROUPS)],
            e_in=jnp.exp(acum64),
            xdw=(xd * jnp.exp(alast - acum64)).astype(BF16),
            e_all=jnp.exp(alast)))

    sts = [state[g] for g in range(SSD_GROUPS)]
    for c in range(nc):
        rows = slice(c * q, (c + 1) * q)
        pc = pre[c]
        y_off = jnp.concatenate([_dot(pc["cgs"][g], sts[g].astype(BF16)) for g in range(SSD_GROUPS)], axis=-1)
        for g in range(SSD_GROUPS):
            gs = slice(g * SSD_GW, (g + 1) * SSD_GW)
            sts[g] = sts[g] * pc["e_all"][:, gs] + _dot(pc["bts"][g], pc["xdw"][:, gs])
        y = (pc["y"] + y_off * pc["e_in"]) * _silu(xz_ref[rows, SSD_CONV_WIDTH:].astype(F32))
        y = y * lax.rsqrt(jnp.mean(y * y, axis=-1, keepdims=True) + LN_EPS) * ng_ref[...]
        out_ref[rows, :] = y.astype(out_ref.dtype)
    for g in range(SSD_GROUPS):
        state[g] = sts[g]


def _ssd_branch(xz, small, prm, layer, bsz, lp):
    q = CHUNK
    rb = _pick(lp, (3 * q, 2 * q, q))
    nblk = lp // rb
    wx = SSD_CONV_WIDTH + BRANCH_WIDTH
    rowmap = lambda b, r: (b * nblk + r, 0)
    lb = functools.partial(_layer_block, layer=layer)
    shift = _shift_matrix(q)
    e64 = _expand_matrix(SM_SSD_DT, SSD_HEADS, SSD_HEAD_DIM, 3)
    e128 = _expand_matrix(SM_SSD_DT, SSD_HEADS, LANES, 3)
    return pl.pallas_call(
        functools.partial(_ssd_kernel, q=q, nc=rb // q),
        grid=(bsz, nblk),
        in_specs=[pl.BlockSpec((rb, wx), rowmap), pl.BlockSpec((rb, LANES), rowmap),
                  _full(shift.shape), lb((CONV_K, SSD_CONV_WIDTH)), lb((1, SSD_CONV_WIDTH)),
                  lb((1, LANES)), lb((1, LANES)), _full(e64.shape), _full(e128.shape),
                  lb((1, BRANCH_WIDTH)), lb((1, BRANCH_WIDTH))],
        out_specs=pl.BlockSpec((rb, BRANCH_WIDTH), rowmap),
        out_shape=jax.ShapeDtypeStruct((bsz * lp, BRANCH_WIDTH), BF16),
        scratch_shapes=[pltpu.VMEM((rb + CONV_CARRY, SSD_CONV_WIDTH), BF16),
                        pltpu.VMEM((CONV_CARRY, SSD_CONV_WIDTH), BF16),
                        pltpu.VMEM((SSD_GROUPS, SSD_STATE, SSD_GW), F32)],
        compiler_params=_params("parallel", "arbitrary"),
        name="ssd_branch",
    )(xz, small, shift, prm["conv_w"], prm["conv_b"], prm["dt_bias"], prm["neg_a"], e64, e128,
      prm["d"], prm["norm_g"])


def _lane_vectors(values, offset):
    nl, n = values.shape
    return jnp.pad(values.astype(F32), ((0, 0), (offset, LANES - offset - n))).reshape(nl, 1, LANES)


def _ssd_prepare(conv_w, conv_b, dt_bias, a_log, d, norm_g):
    nl = conv_w.shape[0]
    return dict(conv_w=conv_w, conv_b=conv_b.reshape(nl, 1, -1),
                dt_bias=_lane_vectors(dt_bias, SM_SSD_DT),
                neg_a=_lane_vectors(-jnp.exp(a_log), SM_SSD_DT),
                d=jnp.repeat(d, SSD_HEAD_DIM, axis=1).reshape(nl, 1, -1), norm_g=norm_g.reshape(nl, 1, -1))


def _head_sums(x, ones_ref):
    outs = []
    for s in range(x.shape[1] // (2 * LANES)):
        xs = x[:, s * 2 * LANES:(s + 1) * 2 * LANES]
        outs.append(_dot(xs.astype(BF16), ones_ref[...]))
    return jnp.concatenate(outs, axis=-1)


def _gdn_kernel(qkvz_ref, sm_ref, shift_ref, cw_ref, dtb_ref, nega_ref, eg_ref, eb_ref, ones_ref, ng_ref,
                out_ref, buf, carry, state, *, q, nc, ns, lr):
    first = pl.program_id(1) == 0

    @pl.when(first)
    def _():
        state[...] = jnp.zeros_like(state)

    for s in range(ns):
        _conv_begin(buf.at[s], carry.at[s], qkvz_ref.at[s], GDN_QKV, first)
    cw = cw_ref[...]

    causal, strict, _ = _tri_masks(q)
    tril = jnp.where(causal, 1.0, 0.0).astype(BF16)

    pre = {}
    for s, c in [(s, c) for s in range(ns) for c in range(nc)]:
        rows = slice(c * q, (c + 1) * q)
        qkv = _silu(_causal_conv(buf.at[s], c, qkvz_ref[s, rows, :GDN_QKV], shift_ref, cw, q))
        qa = qkv[:, :BRANCH_WIDTH]
        ka = qkv[:, BRANCH_WIDTH:2 * BRANCH_WIDTH]
        va = qkv[:, 2 * BRANCH_WIDTH:]
        qa = qa * (lax.rsqrt(_head_sums(qa * qa, ones_ref) + 1e-6) * (GDN_HEAD_DIM ** -0.5))
        ka = ka * lax.rsqrt(_head_sums(ka * ka, ones_ref) + 1e-6)
        sm = sm_ref[s, rows, :]
        glog = nega_ref[...] * _softplus(sm + dtb_ref[...])
        gcum = _cumsum_rows(tril, glog)
        g128 = _dot(_split3(gcum), eg_ref[...])
        beta = _dot(_split2(_sigmoid(sm)), eb_ref[...])
        rid = lax.broadcasted_iota(jnp.int32, beta.shape, 0) + (pl.program_id(1) * (nc * q) + c * q)
        beta = jnp.where(rid < lr, beta, 0.0)
        glast = g128[q - 1:q, :]
        e_in = jnp.exp(g128)
        pre[s, c] = dict(qa=qa, ka=ka, beta=beta, g128=g128, gcum_t=gcum.T, e_all=jnp.exp(glast),
                         vbeta=va * beta, kbe=ka * (beta * e_in), qg=qa * e_in, kd=ka * jnp.exp(glast - g128))

    dk = GDN_HEAD_DIM
    npair = GDN_HEADS // 2
    units = [(s, c, p) for s in range(ns) for c in range(nc) for p in range(npair)]
    ps = [slice(p * 2 * dk, (p + 1) * 2 * dk) for p in range(npair)]
    left = lax.broadcasted_iota(jnp.int32, (q, 2 * dk), 1) < dk

    def blockdiag(x):
        return jnp.concatenate([jnp.where(left, x, 0.0), jnp.where(left, 0.0, x)], axis=0).astype(BF16)

    causal2 = jnp.concatenate([causal, causal], axis=1)
    strict2 = jnp.concatenate([strict, strict], axis=1)
    gammas, pws, tinvs, attns = {}, {}, {}, {}
    for s, c, p in units:
        la = SM_GDN_A + 2 * p
        gt = pre[s, c]["gcum_t"]
        grow = jnp.concatenate([gt[la:la + 1, :], gt[la + 1:la + 2, :]], axis=1)
        gammas[s, c, p] = jnp.where(causal2, jnp.exp(pre[s, c]["g128"][:, ps[p]] - grow), 0.0)
    for s, c, p in units:
        kp = pre[s, c]["ka"][:, ps[p]]
        lhs = jnp.concatenate([kp.astype(BF16), pre[s, c]["qa"][:, ps[p]].astype(BF16)], axis=0)
        kq = _dot_nt(lhs, blockdiag(kp))
        pws[s, c, p] = jnp.where(strict2, -(kq[:q] * gammas[s, c, p] * pre[s, c]["beta"][:, ps[p]]), 0.0)
        attns[s, c, p] = (kq[q:] * gammas[s, c, p]).astype(BF16)
    bfull = dict(pws)
    b = GDN_INV_BASE
    per_head = q // b

    def lane_blocks(bs):
        return lax.broadcasted_iota(jnp.int32, (bs, 2 * q), 1) // bs

    blk = lane_blocks(b)
    eyes = jnp.where(lax.broadcasted_iota(jnp.int32, (b, 2 * q), 1) % b
                     == lax.broadcasted_iota(jnp.int32, (b, 2 * q), 0), 1.0, 0.0)

    def blockdiag_b(x):
        return jnp.concatenate([jnp.where(blk == i, x, 0.0) for i in range(2 * q // b)], axis=0).astype(BF16)

    for u in units:
        diag = bfull[u][:b]
        for j in range(1, per_head):
            diag = jnp.where(blk % per_head == j, bfull[u][j * b:(j + 1) * b], diag)
        tinvs[u] = eyes + diag
        pws[u] = _dot(diag.astype(BF16), blockdiag_b(diag))
    steps = int(math.log2(b)) - 1
    for i in range(steps):
        last = i == steps - 1
        for u in units:
            pwb = pws[u].astype(BF16)
            lhs = tinvs[u].astype(BF16) if last else jnp.concatenate([tinvs[u].astype(BF16), pwb], axis=0)
            both = _dot(lhs, blockdiag_b(pws[u]))
            tinvs[u] = tinvs[u] + both[:b]
            if not last:
                pws[u] = both[b:]
    while b < q:
        nblk = 2 * q // b
        blk = lane_blocks(b)
        even = blk % 2 == 0
        zeros = jnp.zeros((b, 2 * q), F32)
        t2b = {}
        for u in units:
            rows = [zeros if i % 2 == 0 else
                    jnp.where(blk == i - 1, bfull[u][(i % (q // b)) * b:(i % (q // b) + 1) * b], 0.0)
                    for i in range(nblk)]
            t2b[u] = _dot(jnp.where(even, 0.0, tinvs[u]).astype(BF16), jnp.concatenate(rows, axis=0).astype(BF16))
        for u in units:
            td = tinvs[u]
            rows = [jnp.where(blk == i, td, 0.0) if i % 2 == 0 else zeros for i in range(nblk)]
            t21 = _dot(t2b[u].astype(BF16), jnp.concatenate(rows, axis=0).astype(BF16))
            tinvs[u] = jnp.concatenate([jnp.where(even, td, 0.0), jnp.where(even, t21, td)], axis=0)
        b *= 2
    us, ws = {}, {}
    for s, c, p in units:
        uw = []
        for i in range(2):
            hcol = slice((2 * p + i) * dk, (2 * p + i + 1) * dk)
            rhs = jnp.concatenate([pre[s, c]["vbeta"][:, hcol], pre[s, c]["kbe"][:, hcol]], axis=-1)
            uw.append(_dot(tinvs[s, c, p][:, i * dk:(i + 1) * dk].astype(BF16), rhs.astype(BF16)))
        us[s, c, p] = jnp.concatenate([uw[0][:, :dk], uw[1][:, :dk]], axis=-1)
        ws[s, c, p] = jnp.concatenate([uw[0][:, dk:], uw[1][:, dk:]], axis=-1)

    sts = {(s, h): state[s, h] for s in range(ns) for h in range(GDN_HEADS)}
    zero = jnp.zeros((dk, dk), F32)
    for c in range(nc):
        rows = slice(c * q, (c + 1) * q)
        wq = {}
        for s in range(ns):
            for p in range(npair):
                sbd = jnp.concatenate([jnp.concatenate([sts[s, 2 * p], zero], axis=1),
                                       jnp.concatenate([zero, sts[s, 2 * p + 1]], axis=1)], axis=0)
                lhs = jnp.concatenate([ws[s, c, p].astype(BF16), pre[s, c]["qg"][:, ps[p]].astype(BF16)], axis=0)
                wq[s, p] = _dot(lhs, sbd.astype(BF16))
        for s in range(ns):
            outs = []
            for p in range(npair):
                vnew = us[s, c, p] - wq[s, p][:q]
                outs.append(wq[s, p][q:] + _dot(attns[s, c, p], blockdiag(vnew)))
                vb = vnew.astype(BF16)
                for i in range(2):
                    h = 2 * p + i
                    hcol = slice(h * dk, (h + 1) * dk)
                    kdt = pre[s, c]["kd"][:, hcol].T.astype(BF16)
                    sts[s, h] = sts[s, h] * pre[s, c]["e_all"][:, hcol] + _dot(kdt, vb[:, i * dk:(i + 1) * dk])
            o = jnp.concatenate(outs, axis=-1)
            o = o * lax.rsqrt(_head_sums(o * o, ones_ref) * (1.0 / dk) + LN_EPS)
            gate = _silu(qkvz_ref[s, rows, GDN_QKV:].astype(F32))
            o = o * jnp.concatenate([ng_ref[...]] * GDN_HEADS, axis=-1) * gate
            out_ref[s, rows, :] = o.astype(out_ref.dtype)
    for s in range(ns):
        for h in range(GDN_HEADS):
            state[s, h] = sts[s, h]


def _gdn_branch(qkvz, small, prm, layer, bsz, lp, lr):
    q = CHUNK
    assert q == GDN_HEAD_DIM
    rb = _pick(lp, (3 * q, 2 * q, q))
    ns = 1
    wx = GDN_QKV + BRANCH_WIDTH
    rowmap = lambda b, r: (b, r, 0)
    lb = functools.partial(_layer_block, layer=layer)
    shift = _shift_matrix(q)
    e_g = _expand_matrix(SM_GDN_A, GDN_HEADS, GDN_HEAD_DIM, 3)
    e_b = _expand_matrix(SM_GDN_B, GDN_HEADS, GDN_HEAD_DIM, 2)
    ones = _block_ones(2 * LANES, GDN_HEAD_DIM, 1)
    out = pl.pallas_call(
        functools.partial(_gdn_kernel, q=q, nc=rb // q, ns=ns, lr=lr),
        grid=(bsz // ns, lp // rb),
        in_specs=[pl.BlockSpec((ns, rb, wx), rowmap), pl.BlockSpec((ns, rb, LANES), rowmap),
                  _full(shift.shape), lb((CONV_K, GDN_QKV)), lb((1, LANES)), lb((1, LANES)),
                  _full(e_g.shape), _full(e_b.shape), _full(ones.shape), lb((1, GDN_HEAD_DIM))],
        out_specs=pl.BlockSpec((ns, rb, BRANCH_WIDTH), rowmap),
        out_shape=jax.ShapeDtypeStruct((bsz, lp, BRANCH_WIDTH), BF16),
        scratch_shapes=[pltpu.VMEM((ns, rb + CONV_CARRY, GDN_QKV), BF16),
                        pltpu.VMEM((ns, CONV_CARRY, GDN_QKV), BF16),
                        pltpu.VMEM((ns, GDN_HEADS, GDN_HEAD_DIM, GDN_HEAD_DIM), F32)],
        compiler_params=_params("parallel", "arbitrary"),
        name="gdn_branch",
    )(qkvz.reshape(bsz, lp, wx), small.reshape(bsz, lp, LANES), shift, prm["conv_w"], prm["dt_bias"],
      prm["neg_a"], e_g, e_b, ones, prm["norm_g"])
    return out.reshape(bsz * lp, BRANCH_WIDTH)


def _gdn_prepare(conv_w, dt_bias, a_log, norm_g):
    nl = conv_w.shape[0]
    return dict(conv_w=conv_w, dt_bias=_lane_vectors(dt_bias, SM_GDN_A),
                neg_a=_lane_vectors(-jnp.exp(a_log), SM_GDN_A), norm_g=norm_g.reshape(nl, 1, -1))


def _merge_kernel(h_ref, ya_ref, yb_ref, yc_ref, wg_ref, bg_ref, wbr_ref, wo_ref, g_ref, b_ref,
                  o_ref, ob_ref, *, alpha):
    h = h_ref[...]
    logits = _dot(h.astype(BF16), wg_ref[...]) + bg_ref[...]
    merged = None
    for n, y_ref in enumerate((ya_ref, yb_ref, yc_ref)):
        gate = _sigmoid(logits[:, n * D_MODEL:(n + 1) * D_MODEL])
        term = gate * _dot(y_ref[...], wbr_ref[n])
        merged = term if merged is None else merged + term
    y = alpha * h + _dot(merged.astype(BF16), wo_ref[...])
    y = _ln_math(y, g_ref[...], b_ref[...])
    o_ref[...] = y
    ob_ref[...] = y.astype(BF16)


def _merge(h, ya, yb, yc, prm, layer, alpha):
    m, d = h.shape
    tm = _pick(m, (512, 256, 128))
    rows = lambda w: pl.BlockSpec((tm, w), lambda i: (i, 0))
    lb = functools.partial(_layer_block, layer=layer)
    return pl.pallas_call(
        functools.partial(_merge_kernel, alpha=alpha),
        grid=(m // tm,),
        in_specs=[rows(d), rows(BRANCH_WIDTH), rows(BRANCH_WIDTH), rows(BRANCH_WIDTH),
                  lb((d, N_BRANCH * d)), lb((1, N_BRANCH * d)),
                  lb((N_BRANCH, BRANCH_WIDTH, d)), lb((d, d)), lb((1, d)), lb((1, d))],
        out_specs=[rows(d), rows(d)],
        out_shape=[jax.ShapeDtypeStruct((m, d), F32), jax.ShapeDtypeStruct((m, d), BF16)],
        compiler_params=_params("parallel"),
        name="merge",
    )(h, ya, yb, yc, prm["w_gate"], prm["b_gate"], prm["w_branch"], prm["w_out"], prm["ln_g"], prm["ln_b"])


def _split_in_proj(w):
    widths = (BRANCH_WIDTH, BRANCH_WIDTH, SSD_CONV_WIDTH, SSD_HEADS, BRANCH_WIDTH,
              GDN_QKV, GDN_HEADS, GDN_HEADS, BRANCH_WIDTH, N_BRANCH * D_MODEL)
    offs = [0]
    for wd in widths:
        offs.append(offs[-1] + wd)
    w = w.astype(BF16)
    seg = [w[:, :, offs[i]:offs[i + 1]] for i in range(len(widths))]
    s5_u, s5_z, ssd_xbc, ssd_dt, ssd_z, gdn_qkv, gdn_a, gdn_b, gdn_z, gate = seg
    pad = jnp.zeros(w.shape[:2] + (LANES - SSD_HEADS - 2 * GDN_HEADS,), w.dtype)
    return dict(
        s5=jnp.concatenate([s5_u, s5_z], axis=2),
        ssd=jnp.concatenate([ssd_xbc, ssd_z], axis=2),
        gdn=jnp.concatenate([gdn_qkv, gdn_z], axis=2),
        small=jnp.concatenate([ssd_dt, gdn_a, gdn_b, pad], axis=2),
        gate=gate)


def kernel(x, meta, ln_in_g, ln_in_b, w_in, s5_a_re, s5_a_im, s5_log_step, s5_b_re, s5_b_im, s5_c_re, s5_c_im, s5_d, s5_w_glu, s5_b_glu, ssd_conv_w, ssd_conv_b, ssd_dt_bias, ssd_a_log, ssd_d, ssd_norm_g, gdn_conv_w, gdn_dt_bias, gdn_a_log, gdn_norm_g, w_branch, b_gate, w_out, ln_g, ln_b):
    bsz, seq, d = x.shape
    depth = w_in.shape[0]
    alpha = (2 * depth) ** 0.25
    lr = N_META + seq
    lp = -(-lr // CHUNK) * CHUNK
    h0 = jnp.concatenate([jnp.broadcast_to(meta[None].astype(x.dtype), (bsz, N_META, d)), x,
                          jnp.zeros((bsz, lp - lr, d), x.dtype)], axis=1).reshape(bsz * lp, d)
    h, hb = _layer_norm_in(h0, ln_in_g, ln_in_b)

    wi = _split_in_proj(w_in)
    s5p = _s5_prepare(s5_a_re, s5_a_im, s5_log_step, s5_b_re, s5_b_im, s5_c_re, s5_c_im, s5_d, s5_w_glu, s5_b_glu)
    ssdp = _ssd_prepare(ssd_conv_w, ssd_conv_b, ssd_dt_bias, ssd_a_log, ssd_d, ssd_norm_g)
    gdnp = _gdn_prepare(gdn_conv_w, gdn_dt_bias, gdn_a_log, gdn_norm_g)
    mrg = dict(w_gate=wi["gate"], b_gate=b_gate.reshape(depth, 1, -1), w_branch=w_branch.astype(BF16),
               w_out=w_out.astype(BF16), ln_g=ln_g.reshape(depth, 1, -1), ln_b=ln_b.reshape(depth, 1, -1))

    for layer in range(depth):
        p_s5, p_ssd, p_gdn, small = _in_proj(hb, [wi["s5"], wi["ssd"], wi["gdn"], wi["small"]], layer,
                                             [BF16, BF16, BF16, F32])
        y_a = _s5_branch(p_s5, s5p, layer, bsz, lp)
        y_b = _ssd_branch(p_ssd, small, ssdp, layer, bsz, lp)
        y_c = _gdn_branch(p_gdn, small, gdnp, layer, bsz, lp, lr)
        h, hb = _merge(h, y_a, y_b, y_c, mrg, layer, alpha)

    return h.reshape(bsz, lp, d)[:, N_META:lr]
```
